```python
import math
import jax, jax.numpy as jnp
from jax import lax
import numpy as np

D_MODEL = 2048
BATCH = 2
SEQ = 4096
DEPTH = 2
DEC_BATCH = 128
DEC_SEQ = 4
PAST_LEN = 2048
PAGE_SIZE = 128

N_MIXERS = 2
QK_HEAD_DIM = 64
N_HEADS = D_MODEL // (2 * QK_HEAD_DIM)
V_HEAD_DIM = 2 * QK_HEAD_DIM
NUM_BUCKETS = 32
MAX_DISTANCE = 128
Q_BLOCK = 128
CHUNK = 128
D_GATE = D_MODEL
N_GROUPS = 16
GROUP_DIM = D_GATE // N_GROUPS
D_FF = 4 * D_MODEL
N_ATTN_LAYERS = (DEPTH + N_MIXERS - 1) // N_MIXERS
N_CHUNK_LAYERS = DEPTH // N_MIXERS
NORM_EPS = 1e-6
SUBLN_EPS = 1e-5
NEG_INF = -1e30

kernel_name = "hybrid_diffattn_chunkgmlp_step"

F32 = jnp.float32


def _rmsnorm(x, g, eps=NORM_EPS):
    xf = x.astype(F32)
    y = xf * lax.rsqrt(jnp.mean(xf * xf, axis=-1, keepdims=True) + eps)
    return (y * g.astype(F32)).astype(x.dtype)


def _rel_bias(q_pos, k_pos, table):
    n = jnp.maximum(q_pos[:, None] - k_pos[None, :], 0)
    max_exact = NUM_BUCKETS // 2
    nf = jnp.maximum(n, 1).astype(F32)
    large = max_exact + (jnp.log(nf / max_exact) / math.log(MAX_DISTANCE / max_exact)
                         * (NUM_BUCKETS - max_exact)).astype(jnp.int32)
    bucket = jnp.where(n < max_exact, n, jnp.minimum(large, NUM_BUCKETS - 1))
    return jnp.moveaxis(table.astype(F32)[bucket], -1, 0)


def _lambda_init(layer):
    return 0.8 - 0.6 * math.exp(-0.3 * layer)


def _diff_lambda(lam, layer):
    lf = lam.astype(F32)
    return jnp.exp(jnp.sum(lf[0] * lf[1])) - jnp.exp(jnp.sum(lf[2] * lf[3])) + _lambda_init(layer)


def _diff_qkv(h, w_qkv, q_gain, k_gain):
    b, s, _ = h.shape
    q, k, v = jnp.split(h @ w_qkv, 3, axis=-1)
    q = _rmsnorm(q.reshape(b, s, N_HEADS, 2, QK_HEAD_DIM), q_gain)
    k = _rmsnorm(k.reshape(b, s, N_HEADS, 2, QK_HEAD_DIM), k_gain)
    v = v.reshape(b, s, N_HEADS, V_HEAD_DIM)
    return q, k, v


def _diff_out(o, subln_g, w_o, layer):
    b, t = o.shape[:2]
    o = _rmsnorm(o, subln_g, SUBLN_EPS) * (1.0 - _lambda_init(layer))
    return o.reshape(b, t, N_HEADS * V_HEAD_DIM) @ w_o


def diff_attn_prompt(h, w_qkv, w_o, q_gain, k_gain, lam_p, subln_g, rel_table, layer):
    b, s, _ = h.shape
    q, k, v = _diff_qkv(h, w_qkv, q_gain, k_gain)
    lam = _diff_lambda(lam_p, layer)
    scale = QK_HEAD_DIM ** -0.5
    n_blk = s // Q_BLOCK
    k_pos = jnp.arange(s, dtype=jnp.int32)
    q_blocks = jnp.moveaxis(q.reshape(b, n_blk, Q_BLOCK, N_HEADS, 2, QK_HEAD_DIM), 1, 0)
    starts = jnp.arange(n_blk, dtype=jnp.int32) * Q_BLOCK

    def block(args):
        qb, start = args
        q_pos = start + jnp.arange(Q_BLOCK, dtype=jnp.int32)
        logits = jnp.einsum('bqhmd,bkhmd->bhmqk', qb, k).astype(F32) * scale
        logits = logits + _rel_bias(q_pos, k_pos, rel_table)[None, :, None]
        logits = jnp.where(k_pos[None, :] <= q_pos[:, None], logits, NEG_INF)
        p = jax.nn.softmax(logits, axis=-1)
        w = p[:, :, 0] - lam * p[:, :, 1]
        return jnp.einsum('bhqk,bkhd->bqhd', w.astype(v.dtype), v)

    o = lax.map(block, (q_blocks, starts))
    o = jnp.moveaxis(o, 0, 1).reshape(b, s, N_HEADS, V_HEAD_DIM)
    return _diff_out(o, subln_g, w_o, layer), k, v


def diff_attn_sample(h, cache_k, cache_v, page_table, j, w_qkv, w_o, q_gain, k_gain, lam_p,
                     subln_g, rel_table, layer):
    b, t, _ = h.shape
    q, k, v = _diff_qkv(h, w_qkv, q_gain, k_gain)
    lam = _diff_lambda(lam_p, layer)
    scale = QK_HEAD_DIM ** -0.5
    n_pages = page_table.shape[1]
    page = cache_k.shape[2]
    past = n_pages * page
    q_pos = past + jnp.arange(t, dtype=jnp.int32)
    past_pos = jnp.arange(past, dtype=jnp.int32)
    page_ids = jnp.arange(n_pages, dtype=jnp.int32)

    def page_logits(p):
        kp = cache_k[j, page_table[:, p]]
        return jnp.einsum('bqhmd,bkhmd->bhmqk', q, kp).astype(F32)

    lp = lax.map(page_logits, page_ids)
    lp = jnp.moveaxis(lp, 0, -2).reshape(b, N_HEADS, 2, t, past) * scale
    lp = lp + _rel_bias(q_pos, past_pos, rel_table)[None, :, None]
    ln = jnp.einsum('bqhmd,bkhmd->bhmqk', q, k).astype(F32) * scale
    ln = ln + _rel_bias(q_pos, q_pos, rel_table)[None, :, None]
    ln = jnp.where(jnp.tril(jnp.ones((t, t), dtype=bool)), ln, NEG_INF)
    p = jax.nn.softmax(jnp.concatenate([lp, ln], axis=-1), axis=-1)
    w = (p[:, :, 0] - lam * p[:, :, 1]).astype(v.dtype)
    w_past = jnp.moveaxis(w[..., :past].reshape(b, N_HEADS, t, n_pages, page), 3, 0)

    def page_values(args):
        pid, wp = args
        vp = cache_v[j, page_table[:, pid]]
        return jnp.einsum('bhqk,bkhd->bqhd', wp, vp).astype(F32)

    o = lax.map(page_values, (page_ids, w_past)).sum(axis=0)
    o = o + jnp.einsum('bhqk,bkhd->bqhd', w[..., past:], v).astype(F32)
    return _diff_out(o.astype(v.dtype), subln_g, w_o, layer), k, v


def _chunk_proj(h, w_in, v_gain):
    b, s, _ = h.shape
    u, v = jnp.split(jax.nn.gelu(h @ w_in, approximate=False), 2, axis=-1)
    v = _rmsnorm(v.reshape(b, s, N_GROUPS, GROUP_DIM), v_gain.reshape(N_GROUPS, GROUP_DIM))
    return u, v


def _causal_spatial(w_s):
    return w_s * jnp.tril(jnp.ones((CHUNK, CHUNK), w_s.dtype))


def chunk_mlp_prompt(h, w_in, v_gain, w_s, b_s, w_o):
    b, s, _ = h.shape
    u, v = _chunk_proj(h, w_in, v_gain)
    vc = v.reshape(b, s // CHUNK, CHUNK, N_GROUPS, GROUP_DIM)
    mixed = jnp.einsum('gts,bcsgd->bctgd', _causal_spatial(w_s), vc) + b_s.T[:, :, None]
    return (u * mixed.reshape(b, s, D_GATE)) @ w_o


def chunk_mlp_sample(h, w_in, v_gain, w_s, b_s, w_o):
    b, t, _ = h.shape
    u, v = _chunk_proj(h, w_in, v_gain)
    ws = _causal_spatial(w_s)[:, :t, :t]
    mixed = jnp.einsum('gts,bsgd->btgd', ws, v) + b_s[:, :t].T[:, :, None]
    return (u * mixed.reshape(b, t, D_GATE)) @ w_o, v


def _sqrelu_ffn(h, w1, w2):
    a = jax.nn.relu(h @ w1)
    return (a * a) @ w2


def setup_inputs(seed: int = 0) -> dict:
    key = jax.random.key(seed)
    ks = jax.random.split(key, 24)
    n_pages = PAST_LEN // PAGE_SIZE
    n_used = DEC_BATCH * n_pages
    n_phys = n_used + max(1, n_used // 4)
    nrm = lambda k, shape, scale: jax.random.normal(k, shape, F32) * scale
    gain = lambda k, shape: 1.0 + 0.05 * jax.random.normal(k, shape, F32)
    page_table = jax.random.permutation(ks[0], n_phys)[:n_used].reshape(DEC_BATCH, n_pages).astype(jnp.int32)
    return {
        "x_prompt": nrm(ks[1], (BATCH, SEQ, D_MODEL), 1.0),
        "x_sample": nrm(ks[2], (DEC_BATCH, DEC_SEQ, D_MODEL), 1.0),
        "cache_k": nrm(ks[3], (N_ATTN_LAYERS, n_phys, PAGE_SIZE, N_HEADS, 2, QK_HEAD_DIM), 1.0),
        "cache_v": nrm(ks[4], (N_ATTN_LAYERS, n_phys, PAGE_SIZE, N_HEADS, V_HEAD_DIM), 1.0),
        "page_table": page_table,
        "rel_bias_table": nrm(ks[5], (NUM_BUCKETS, N_HEADS), 0.5),
        "mix_norm": gain(ks[6], (DEPTH, D_MODEL)),
        "ffn_norm": gain(ks[7], (DEPTH, D_MODEL)),
        "attn_w_qkv": nrm(ks[8], (N_ATTN_LAYERS, D_MODEL, 3 * D_MODEL), D_MODEL ** -0.5),
        "attn_w_o": nrm(ks[9], (N_ATTN_LAYERS, N_HEADS * V_HEAD_DIM, D_MODEL), D_MODEL ** -0.5),
        "attn_q_norm": gain(ks[10], (N_ATTN_LAYERS, QK_HEAD_DIM)),
        "attn_k_norm": gain(ks[11], (N_ATTN_LAYERS, QK_HEAD_DIM)),
        "attn_lambda": nrm(ks[12], (N_ATTN_LAYERS, 4, QK_HEAD_DIM), 0.1),
        "attn_subln": gain(ks[13], (N_ATTN_LAYERS, V_HEAD_DIM)),
        "cm_w_in": nrm(ks[14], (N_CHUNK_LAYERS, D_MODEL, 2 * D_GATE), D_MODEL ** -0.5),
        "cm_v_norm": gain(ks[15], (N_CHUNK_LAYERS, D_GATE)),
        "cm_w_s": nrm(ks[16], (N_CHUNK_LAYERS, N_GROUPS, CHUNK, CHUNK), CHUNK ** -0.5),
        "cm_b_s": 1.0 + nrm(ks[17], (N_CHUNK_LAYERS, N_GROUPS, CHUNK), 0.1),
        "cm_w_o": nrm(ks[18], (N_CHUNK_LAYERS, D_GATE, D_MODEL), D_GATE ** -0.5),
        "ffn_w1": nrm(ks[19], (DEPTH, D_MODEL, D_FF), D_MODEL ** -0.5),
        "ffn_w2": nrm(ks[20], (DEPTH, D_FF, D_MODEL), D_FF ** -0.5),
    }


def reference(x_prompt, x_sample, cache_k, cache_v, page_table, rel_bias_table, mix_norm, ffn_norm,
              attn_w_qkv, attn_w_o, attn_q_norm, attn_k_norm, attn_lambda, attn_subln,
              cm_w_in, cm_v_norm, cm_w_s, cm_b_s, cm_w_o, ffn_w1, ffn_w2):
    xp, xs = x_prompt, x_sample
    k_prompt, v_prompt, k_sample, v_sample, cv_sample = [], [], [], [], []
    for i in range(DEPTH):
        j = i // N_MIXERS
        hp = _rmsnorm(xp, mix_norm[i])
        hs = _rmsnorm(xs, mix_norm[i])
        if i % N_MIXERS == 0:
            op, kp, vp = diff_attn_prompt(hp, attn_w_qkv[j], attn_w_o[j], attn_q_norm[j], attn_k_norm[j],
                                          attn_lambda[j], attn_subln[j], rel_bias_table, i)
            os_, ks_, vs_ = diff_attn_sample(hs, cache_k, cache_v, page_table, j, attn_w_qkv[j], attn_w_o[j],
                                             attn_q_norm[j], attn_k_norm[j], attn_lambda[j], attn_subln[j],
                                             rel_bias_table, i)
            k_prompt.append(kp)
            v_prompt.append(vp)
            k_sample.append(ks_)
            v_sample.append(vs_)
        else:
            op = chunk_mlp_prompt(hp, cm_w_in[j], cm_v_norm[j], cm_w_s[j], cm_b_s[j], cm_w_o[j])
            os_, cv = chunk_mlp_sample(hs, cm_w_in[j], cm_v_norm[j], cm_w_s[j], cm_b_s[j], cm_w_o[j])
            cv_sample.append(cv)
        xp = xp + op
        xs = xs + os_
        xp = xp + _sqrelu_ffn(_rmsnorm(xp, ffn_norm[i]), ffn_w1[i], ffn_w2[i])
        xs = xs + _sqrelu_ffn(_rmsnorm(xs, ffn_norm[i]), ffn_w1[i], ffn_w2[i])
    return (xp, xs, jnp.stack(k_prompt), jnp.stack(v_prompt), jnp.stack(k_sample), jnp.stack(v_sample),
            jnp.stack(cv_sample))
```

```python
import functools
import math

import jax
import jax.numpy as jnp
from jax import lax
from jax.experimental import pallas as pl
from jax.experimental.pallas import tpu as pltpu

F32 = jnp.float32
BF16 = jnp.bfloat16

QK_HEAD_DIM = 64
V_HEAD_DIM = 128
NUM_BUCKETS = 32
MAX_DISTANCE = 128
CHUNK = 128
GROUP_DIM = 128
N_MIXERS = 2
NORM_EPS = 1e-6
SUBLN_EPS = 1e-5
NEG_INF = -1e30

LANES = 128
MXU_DIM = 256
VMEM_CAP = 56 * 1024 * 1024


def _lambda_init(layer):
    return 0.8 - 0.6 * math.exp(-0.3 * layer)


def _params(semantics, vmem_bytes):
    limit = int(min(VMEM_CAP, max(32 * 1024 * 1024, vmem_bytes * 5 // 4)))
    return pltpu.CompilerParams(dimension_semantics=semantics, vmem_limit_bytes=limit)


def _rms_rows(x, g, eps):
    ms = jnp.mean(x * x, axis=-1, keepdims=True)
    return x * lax.rsqrt(ms + eps) * g


def _group_rms(x, bsum, group, eps):
    x2 = x * x
    hi = x2.astype(BF16)
    lo = (x2 - hi.astype(F32)).astype(BF16)
    parts = []
    for c in range(x.shape[1] // MXU_DIM):
        sl = slice(c * MXU_DIM, (c + 1) * MXU_DIM)
        parts.append(jnp.dot(hi[:, sl], bsum, preferred_element_type=F32)
                     + jnp.dot(lo[:, sl], bsum, preferred_element_type=F32))
    ss = parts[0] if len(parts) == 1 else jnp.concatenate(parts, axis=1)
    return x * lax.rsqrt(ss * (1.0 / group) + eps)


def _group_sum_matrix(group):
    r = jnp.arange(MXU_DIM) // group
    return (r[:, None] == r[None, :]).astype(BF16)


def _diff_lambda_value(lam_ref, layer):
    lf = lam_ref[...]
    a = jnp.sum(lf[0:1] * lf[1:2], axis=-1, keepdims=True)
    b = jnp.sum(lf[2:3] * lf[3:4], axis=-1, keepdims=True)
    return jnp.exp(a) - jnp.exp(b) + _lambda_init(layer)


def _qkv_kernel(x_ref, g_ref, wq_ref, wk_ref, wv_ref, qg_ref, kg_ref, bsum_ref,
                q_ref, k_ref, v_ref, kb_ref, vb_ref, h_scr):
    @pl.when(pl.program_id(1) == 0)
    def _():
        h_scr[...] = _rms_rows(x_ref[...], g_ref[...], NORM_EPS).astype(BF16)

    h = h_scr[...]
    bsum = bsum_ref[...]
    q = jnp.dot(h, wq_ref[...], preferred_element_type=F32)
    qn = _group_rms(q, bsum, QK_HEAD_DIM, NORM_EPS) * qg_ref[...]
    q_ref[...] = (qn * (QK_HEAD_DIM ** -0.5)).astype(BF16)
    k = jnp.dot(h, wk_ref[...], preferred_element_type=F32)
    kn = _group_rms(k, bsum, QK_HEAD_DIM, NORM_EPS) * kg_ref[...]
    k_ref[...] = kn
    kb_ref[...] = kn.astype(BF16)
    v = jnp.dot(h, wv_ref[...], preferred_element_type=F32)
    v_ref[...] = v
    vb_ref[...] = v.astype(BF16)


def _qkv(x, g, w_qkv, q_gain, k_gain, tm, tn):
    m, d = x.shape
    n = w_qkv.shape[1] // 3
    nj = n // tn
    qg = jnp.tile(q_gain, tn // QK_HEAD_DIM)[None, :]
    kg = jnp.tile(k_gain, tn // QK_HEAD_DIM)[None, :]
    bsum = _group_sum_matrix(QK_HEAD_DIM)
    row = pl.BlockSpec((tm, d), lambda i, j: (i, 0))
    col = lambda off: pl.BlockSpec((d, tn), lambda i, j, off=off: (0, j + off))
    vec = lambda width: pl.BlockSpec((1, width), lambda i, j: (0, 0))
    out = pl.BlockSpec((tm, tn), lambda i, j: (i, j))
    vmem = 2 * tm * d * 4 + tm * d * 2 + 6 * d * tn * 2 + 2 * tm * tn * (2 + 4 + 4 + 2 + 2) + 8 * tm * tn * 4
    return pl.pallas_call(
        _qkv_kernel,
        grid=(m // tm, nj),
        in_specs=[row, vec(d), col(0), col(nj), col(2 * nj), vec(tn), vec(tn),
                  pl.BlockSpec((MXU_DIM, MXU_DIM), lambda i, j: (0, 0))],
        out_specs=[out, out, out, out, out],
        out_shape=[jax.ShapeDtypeStruct((m, n), BF16), jax.ShapeDtypeStruct((m, n), F32),
                   jax.ShapeDtypeStruct((m, n), F32), jax.ShapeDtypeStruct((m, n), BF16),
                   jax.ShapeDtypeStruct((m, n), BF16)],
        scratch_shapes=[pltpu.VMEM((tm, d), BF16)],
        compiler_params=_params(("parallel", "arbitrary"), vmem),
        name="qkv",
    )(x, g[None, :], w_qkv, w_qkv, w_qkv, qg, kg, bsum)


def _mm_res_kernel(x_ref, a_ref, w_ref, o_ref):
    o_ref[...] = x_ref[...] + jnp.dot(a_ref[...].astype(BF16), w_ref[...],
                                      preferred_element_type=F32)


def _mm_res(x, a, w, tm, tn):
    m, n = x.shape
    kd = a.shape[1]
    vmem = 2 * tm * kd * a.dtype.itemsize + 2 * kd * tn * 2 + 4 * tm * tn * 4 + tm * kd * 2 + tm * tn * 4
    return pl.pallas_call(
        _mm_res_kernel,
        grid=(m // tm, n // tn),
        in_specs=[pl.BlockSpec((tm, tn), lambda i, j: (i, j)),
                  pl.BlockSpec((tm, kd), lambda i, j: (i, 0)),
                  pl.BlockSpec((kd, tn), lambda i, j: (0, j))],
        out_specs=pl.BlockSpec((tm, tn), lambda i, j: (i, j)),
        out_shape=jax.ShapeDtypeStruct((m, n), F32),
        compiler_params=_params(("parallel", "arbitrary"), vmem),
        name="mm_res",
    )(x, a, w)


def _ffn_kernel(x_ref, g_ref, w1_ref, w2_ref, o_ref, h_scr):
    @pl.when(pl.program_id(1) == 0)
    def _():
        x = x_ref[...]
        h_scr[...] = _rms_rows(x, g_ref[...], NORM_EPS).astype(BF16)
        o_ref[...] = x

    a = jnp.dot(h_scr[...], w1_ref[...], preferred_element_type=F32)
    a = jnp.maximum(a, 0.0)
    a = (a * a).astype(BF16)
    o_ref[...] += jnp.dot(a, w2_ref[...], preferred_element_type=F32)


def _ffn(x, g, w1, w2, tm, tf):
    m, d = x.shape
    dff = w1.shape[1]
    vmem = 4 * tm * d * 4 + tm * d * 2 + 4 * d * tf * 2 + tm * tf * 8
    return pl.pallas_call(
        _ffn_kernel,
        grid=(m // tm, dff // tf),
        in_specs=[pl.BlockSpec((tm, d), lambda i, f: (i, 0)),
                  pl.BlockSpec((1, d), lambda i, f: (0, 0)),
                  pl.BlockSpec((d, tf), lambda i, f: (0, f)),
                  pl.BlockSpec((tf, d), lambda i, f: (f, 0))],
        out_specs=pl.BlockSpec((tm, d), lambda i, f: (i, 0)),
        out_shape=jax.ShapeDtypeStruct((m, d), F32),
        scratch_shapes=[pltpu.VMEM((tm, d), BF16)],
        compiler_params=_params(("parallel", "arbitrary"), vmem),
        name="ffn",
    )(x, g[None, :], w1, w2)


def _gelu(x):
    return 0.5 * x * (1.0 + lax.erf(x * (2.0 ** -0.5)))


def _gmlp_kernel(x_ref, g_ref, wu_ref, wv_ref, vg_ref, bsum_ref, ws_ref, bs_ref, *rest, emit_cv):
    if emit_cv:
        gate_ref, cv_ref, h_scr = rest
    else:
        gate_ref, h_scr = rest

    @pl.when(pl.program_id(1) == 0)
    def _():
        h_scr[...] = _rms_rows(x_ref[...], g_ref[...], NORM_EPS).astype(BF16)

    h = h_scr[...]
    tm = h.shape[0]
    tn = wu_ref.shape[1]
    u = _gelu(jnp.dot(h, wu_ref[...], preferred_element_type=F32))
    v = _gelu(jnp.dot(h, wv_ref[...], preferred_element_type=F32))
    vn = _group_rms(v, bsum_ref[...], GROUP_DIM, NORM_EPS) * vg_ref[...]
    if emit_cv:
        cv_ref[...] = vn
    vb = vn.astype(BF16)
    t_idx = lax.broadcasted_iota(jnp.int32, (CHUNK, CHUNK), 0)
    s_idx = lax.broadcasted_iota(jnp.int32, (CHUNK, CHUNK), 1)
    causal = t_idx >= s_idx
    for gi in range(tn // GROUP_DIM):
        cols = slice(gi * GROUP_DIM, (gi + 1) * GROUP_DIM)
        w = jnp.where(causal, ws_ref[gi], 0.0).astype(BF16)
        bias = bs_ref[gi]
        for c in range(tm // CHUNK):
            rows = slice(c * CHUNK, (c + 1) * CHUNK)
            mixed = jnp.dot(w, vb[rows, cols], preferred_element_type=F32) + bias
            gate_ref[rows, cols] = (u[rows, cols] * mixed).astype(BF16)


def _gmlp_front(x, g, w_in, v_gain, ws, bs, tm, tn, emit_cv):
    m, d = x.shape
    dg = w_in.shape[1] // 2
    nj = dg // tn
    gpt = tn // GROUP_DIM
    blk = pl.BlockSpec((tm, tn), lambda i, j: (i, j))
    out_specs = [blk, blk] if emit_cv else [blk]
    out_shape = [jax.ShapeDtypeStruct((m, dg), BF16)]
    if emit_cv:
        out_shape.append(jax.ShapeDtypeStruct((m, dg), F32))
    vmem = 2 * tm * d * 4 + tm * d * 2 + 4 * d * tn * 2 + 4 * gpt * CHUNK * CHUNK * 4 + 12 * tm * tn * 4
    res = pl.pallas_call(
        functools.partial(_gmlp_kernel, emit_cv=emit_cv),
        grid=(m // tm, nj),
        in_specs=[pl.BlockSpec((tm, d), lambda i, j: (i, 0)),
                  pl.BlockSpec((1, d), lambda i, j: (0, 0)),
                  pl.BlockSpec((d, tn), lambda i, j: (0, j)),
                  pl.BlockSpec((d, tn), lambda i, j, nj=nj: (0, j + nj)),
                  pl.BlockSpec((1, tn), lambda i, j: (0, j)),
                  pl.BlockSpec((MXU_DIM, MXU_DIM), lambda i, j: (0, 0)),
                  pl.BlockSpec((gpt, CHUNK, CHUNK), lambda i, j: (j, 0, 0)),
                  pl.BlockSpec((gpt, CHUNK, GROUP_DIM), lambda i, j: (j, 0, 0))],
        out_specs=out_specs,
        out_shape=out_shape,
        scratch_shapes=[pltpu.VMEM((tm, d), BF16)],
        compiler_params=_params(("parallel", "arbitrary"), vmem),
        name="gmlp_front",
    )(x, g[None, :], w_in, w_in, v_gain[None, :], _group_sum_matrix(GROUP_DIM), ws, bs)
    return res if emit_cv else (res[0], None)


def _shifted_bias(dist, table_ref, head):
    max_exact = NUM_BUCKETS // 2
    n = jnp.maximum(dist, 0)
    nf = jnp.maximum(n, 1).astype(F32)
    large = max_exact + (jnp.log(nf / max_exact) / math.log(MAX_DISTANCE / max_exact)
                         * (NUM_BUCKETS - max_exact)).astype(jnp.int32)
    bucket = jnp.where(n < max_exact, n, jnp.minimum(large, NUM_BUCKETS - 1))
    far = table_ref[NUM_BUCKETS - 1, head]
    acc = jnp.zeros(dist.shape, F32)
    for b in range(NUM_BUCKETS - 1):
        acc = jnp.where(bucket == b, table_ref[b, head] - far, acc)
    return acc


def _prompt_bias_kernel(table_ref, o_ref, *, tile):
    head = pl.program_id(0)
    r = lax.broadcasted_iota(jnp.int32, (LANES, LANES), 0)
    c = lax.broadcasted_iota(jnp.int32, (LANES, LANES), 1)
    d0 = jnp.where(r >= c, _shifted_bias(r - c, table_ref, head), NEG_INF)
    d1 = _shifted_bias(LANES + r - c, table_ref, head)
    nb = tile // LANES
    zeros = jnp.zeros((LANES, LANES), F32)
    masked = jnp.full((LANES, LANES), NEG_INF, F32)
    for a in range(nb):
        for b in range(nb):
            rows = slice(a * LANES, (a + 1) * LANES)
            cols = slice(b * LANES, (b + 1) * LANES)
            diag = d0 if a == b else d1 if a == b + 1 else zeros if a > b else masked
            o_ref[0, 0, rows, cols] = diag
            o_ref[0, 1, rows, cols] = d1 if (a == 0 and b == nb - 1) else zeros


def _prompt_bias(table, n_heads, tile):
    return pl.pallas_call(
        functools.partial(_prompt_bias_kernel, tile=tile),
        grid=(n_heads,),
        in_specs=[pl.BlockSpec(memory_space=pltpu.SMEM)],
        out_specs=pl.BlockSpec((1, 2, tile, tile), lambda h: (h, 0, 0, 0)),
        out_shape=jax.ShapeDtypeStruct((n_heads, 2, tile, tile), F32),
        compiler_params=_params(("parallel",), 8 * tile * tile * 4),
        name="prompt_bias",
    )(table)


def _sample_bias_kernel(table_ref, last_ref, new_ref, *, n_heads, dec_seq, page):
    rows = 2 * n_heads * dec_seq
    ridx = lax.broadcasted_iota(jnp.int32, (rows, page), 0)
    key = lax.broadcasted_iota(jnp.int32, (rows, page), 1)
    tok = ridx % dec_seq
    row_head = (ridx % (n_heads * dec_seq)) // dec_seq
    last = jnp.zeros((rows, page), F32)
    new = jnp.zeros((rows, page), F32)
    for h in range(n_heads):
        last = jnp.where(row_head == h, _shifted_bias(page + tok - key, table_ref, h), last)
        new = jnp.where(row_head == h, _shifted_bias(tok - key, table_ref, h), new)
    last_ref[...] = last
    new_ref[...] = jnp.where(key <= tok, new, NEG_INF)


def _sample_bias(table, n_heads, dec_seq, page):
    rows = 2 * n_heads * dec_seq
    return pl.pallas_call(
        functools.partial(_sample_bias_kernel, n_heads=n_heads, dec_seq=dec_seq, page=page),
        in_specs=[pl.BlockSpec(memory_space=pltpu.SMEM)],
        out_specs=[pl.BlockSpec(memory_space=pltpu.VMEM), pl.BlockSpec(memory_space=pltpu.VMEM)],
        out_shape=[jax.ShapeDtypeStruct((rows, page), F32), jax.ShapeDtypeStruct((rows, page), F32)],
        name="sample_bias",
    )(table)


def _attn_prompt_kernel(it_ref, jt_ref, q_ref, k_ref, v_ref, bias_ref, lam_ref, sg_ref, o_ref,
                        qs_scr, m_scr, l_scr, acc_scr, *, layer):
    t = pl.program_id(2)
    i = it_ref[t]
    j = jt_ref[t]
    tq = q_ref.shape[0]

    @pl.when(j == 0)
    def _():
        q = q_ref[...]
        lane = lax.broadcasted_iota(jnp.int32, q.shape, 1)
        zero = jnp.zeros_like(q)
        qs_scr[0:tq, :] = jnp.where(lane < QK_HEAD_DIM, q, zero)
        qs_scr[tq:2 * tq, :] = jnp.where(lane >= QK_HEAD_DIM, q, zero)
        m_scr[...] = jnp.full(m_scr.shape, NEG_INF, F32)
        l_scr[...] = jnp.zeros(l_scr.shape, F32)
        acc_scr[...] = jnp.zeros(acc_scr.shape, F32)

    def step(bias):
        s = lax.dot_general(qs_scr[...], k_ref[...], (((1,), (1,)), ((), ())),
                            preferred_element_type=F32)
        if bias is not None:
            s = s + jnp.concatenate([bias, bias], axis=0)
        m_prev = m_scr[...]
        m_new = jnp.maximum(m_prev, jnp.max(s, axis=-1, keepdims=True))
        alpha = jnp.exp(m_prev - m_new)
        p = jnp.exp(s - m_new)
        l_scr[...] = alpha * l_scr[...] + jnp.sum(p, axis=-1, keepdims=True)
        acc_scr[...] = alpha * acc_scr[...] + jnp.dot(p.astype(BF16), v_ref[...],
                                                      preferred_element_type=F32)
        m_scr[...] = m_new

    @pl.when(j < i - 1)
    def _():
        step(None)

    @pl.when(j == i - 1)
    def _():
        step(bias_ref[0, 1])

    @pl.when(j == i)
    def _():
        step(bias_ref[0, 0])
        lam = _diff_lambda_value(lam_ref, layer)
        o = acc_scr[...] / l_scr[...]
        od = o[0:tq, :] - lam * o[tq:2 * tq, :]
        on = _rms_rows(od, sg_ref[...], SUBLN_EPS) * (1.0 - _lambda_init(layer))
        o_ref[...] = on.astype(BF16)


def _attn_prompt(q, kb, vb, bias, lam, subln, batch, tile, layer):
    m, d = q.shape
    n_heads = d // V_HEAD_DIM
    nq = m // batch // tile
    pairs = [(i, j) for i in range(nq) for j in range(i + 1)]
    it = jnp.asarray([p[0] for p in pairs], jnp.int32)
    jt = jnp.asarray([p[1] for p in pairs], jnp.int32)
    qmap = lambda b, h, t, it_ref, jt_ref: (b * nq + it_ref[t], h)
    kmap = lambda b, h, t, it_ref, jt_ref: (b * nq + jt_ref[t], h)
    grid_spec = pltpu.PrefetchScalarGridSpec(
        num_scalar_prefetch=2,
        grid=(batch, n_heads, len(pairs)),
        in_specs=[pl.BlockSpec((tile, V_HEAD_DIM), qmap),
                  pl.BlockSpec((tile, V_HEAD_DIM), kmap),
                  pl.BlockSpec((tile, V_HEAD_DIM), kmap),
                  pl.BlockSpec((1, 2, tile, tile), lambda b, h, t, it_ref, jt_ref: (h, 0, 0, 0)),
                  pl.BlockSpec(lam.shape, lambda b, h, t, it_ref, jt_ref: (0, 0)),
                  pl.BlockSpec((1, V_HEAD_DIM), lambda b, h, t, it_ref, jt_ref: (0, 0))],
        out_specs=pl.BlockSpec((tile, V_HEAD_DIM), qmap),
        scratch_shapes=[pltpu.VMEM((2 * tile, V_HEAD_DIM), BF16),
                        pltpu.VMEM((2 * tile, 1), F32),
                        pltpu.VMEM((2 * tile, 1), F32),
                        pltpu.VMEM((2 * tile, V_HEAD_DIM), F32)],
    )
    vmem = 4 * tile * tile * 4 + 8 * tile * tile * 4 + 6 * tile * LANES * 4 * 2
    return pl.pallas_call(
        functools.partial(_attn_prompt_kernel, layer=layer),
        grid_spec=grid_spec,
        out_shape=jax.ShapeDtypeStruct((m, d), BF16),
        compiler_params=_params(("parallel", "parallel", "arbitrary"), vmem),
        name="attn_prompt",
    )(it, jt, q, kb, vb, bias, lam, subln[None, :])


def _attn_sample_kernel(pt_ref, q_ref, kn_ref, vn_ref, kc_ref, vc_ref, blast_ref, bnew_ref, lam_ref,
                        sg_ref, o_ref, qbd_scr, lg_scr, w_scr, wn_scr, acc_scr, *, layer, n_pages, dec_seq):
    s = pl.program_id(1)
    d = q_ref.shape[2]
    n_heads = d // V_HEAD_DIM
    rows = 2 * n_heads * dec_seq
    half = rows // 2
    page = kc_ref.shape[1]
    group_rows = acc_scr.shape[1]
    group_heads = group_rows // dec_seq

    def padded_new(ref):
        x = ref[0]
        return jnp.concatenate([x, jnp.zeros((page - x.shape[0], d), F32)], axis=0).astype(BF16)

    @pl.when(s == 0)
    def _():
        q = q_ref[0]
        qrep = jnp.broadcast_to(q[None], (rows // q.shape[0], q.shape[0], d)).reshape(rows, d)
        ridx = lax.broadcasted_iota(jnp.int32, (rows, d), 0)
        lane = lax.broadcasted_iota(jnp.int32, (rows, d), 1)
        row_head = (ridx % half) // dec_seq
        row_map = ridx // half
        match = (lane // V_HEAD_DIM == row_head) & ((lane // QK_HEAD_DIM) % 2 == row_map)
        qbd_scr[...] = jnp.where(match, qrep, 0.0).astype(BF16)
        acc_scr[...] = jnp.zeros(acc_scr.shape, F32)

    @pl.when(s < n_pages)
    def _():
        lg_scr[s] = jnp.dot(qbd_scr[...], kc_ref[...].astype(BF16),
                            preferred_element_type=F32)

    @pl.when(s == n_pages - 1)
    def _():
        lam = _diff_lambda_value(lam_ref, layer)
        lg_scr[n_pages - 1] = lg_scr[n_pages - 1] + blast_ref[...]
        lgn = lax.dot_general(qbd_scr[...], padded_new(kn_ref), (((1,), (1,)), ((), ())),
                              preferred_element_type=F32) + bnew_ref[...]
        lg = lg_scr[...]
        m = jnp.maximum(jnp.max(jnp.max(lg, axis=0), axis=-1, keepdims=True),
                        jnp.max(lgn, axis=-1, keepdims=True))
        p = jnp.exp(lg - m[None])
        pn = jnp.exp(lgn - m)
        l = jnp.sum(jnp.sum(p, axis=0), axis=-1, keepdims=True) + jnp.sum(pn, axis=-1, keepdims=True)
        inv = 1.0 / l
        p = p * inv[None]
        pn = pn * inv
        w_scr[...] = p[:, 0:half, :] - lam * p[:, half:rows, :]
        wn_scr[...] = (pn[0:half, :] - lam * pn[half:rows, :]).astype(BF16)

    @pl.when(s >= n_pages)
    def _():
        for g in range(n_heads // group_heads):
            v = jnp.concatenate(
                [vc_ref[pl.ds(g * group_heads + e, page, stride=n_heads), :].astype(BF16)
                 for e in range(group_heads)], axis=1)
            w = w_scr[s - n_pages, g * group_rows:(g + 1) * group_rows, :].astype(BF16)
            acc_scr[g] += jnp.dot(w, v, preferred_element_type=F32)

    @pl.when(s == 2 * n_pages - 1)
    def _():
        new = jnp.dot(wn_scr[...], padded_new(vn_ref), preferred_element_type=F32)
        scale = 1.0 - _lambda_init(layer)
        for h in range(n_heads):
            g, e = divmod(h, group_heads)
            cols = slice(h * V_HEAD_DIM, (h + 1) * V_HEAD_DIM)
            piece = (acc_scr[g, e * dec_seq:(e + 1) * dec_seq, e * V_HEAD_DIM:(e + 1) * V_HEAD_DIM]
                     + new[h * dec_seq:(h + 1) * dec_seq, cols])
            o_ref[0, :, cols] = _rms_rows(piece, sg_ref[...], SUBLN_EPS) * scale


def _attn_sample(q, k_new, v_new, cache_k, cache_v, page_table, blast, bnew, lam, subln, layer):
    batch, dec_seq, d = q.shape
    n_pages = page_table.shape[1]
    page = cache_k.shape[2]
    n_heads = d // V_HEAD_DIM
    rows = 2 * n_heads * dec_seq
    sub_f32, sub_bf16 = 8, 16
    assert sub_f32 % dec_seq == 0
    q = jnp.tile(q, (1, sub_f32 // dec_seq, 1))
    k_new = jnp.pad(k_new, ((0, 0), (0, sub_bf16 - dec_seq), (0, 0)))
    v_new = jnp.pad(v_new, ((0, 0), (0, sub_bf16 - dec_seq), (0, 0)))
    tok = lambda n: pl.BlockSpec((1, n, d), lambda b, s, pt: (b, 0, 0))
    full = lambda a: pl.BlockSpec(a.shape, lambda b, s, pt: (0,) * a.ndim)
    grid_spec = pltpu.PrefetchScalarGridSpec(
        num_scalar_prefetch=1,
        grid=(batch, 2 * n_pages),
        in_specs=[tok(sub_f32), tok(sub_bf16), tok(sub_bf16),
                  pl.BlockSpec((None, d, page),
                               lambda b, s, pt: (pt[b, jnp.minimum(s, n_pages - 1)], 0, 0)),
                  pl.BlockSpec((None, page * n_heads, V_HEAD_DIM),
                               lambda b, s, pt: (pt[b, jnp.maximum(s - n_pages, 0)], 0, 0)),
                  full(blast), full(bnew), full(lam),
                  pl.BlockSpec((1, V_HEAD_DIM), lambda b, s, pt: (0, 0))],
        out_specs=pl.BlockSpec((1, dec_seq, d), lambda b, s, pt: (b, 0, 0)),
        scratch_shapes=[pltpu.VMEM((rows, d), BF16),
                        pltpu.VMEM((n_pages, rows, page), F32),
                        pltpu.VMEM((n_pages, rows // 2, page), F32),
                        pltpu.VMEM((rows // 2, page), BF16),
                        pltpu.VMEM((n_heads * dec_seq // sub_f32, sub_f32,
                                    sub_f32 // dec_seq * V_HEAD_DIM), F32)],
    )
    vmem = 4 * page * d * 4 + 2 * page * d * 2 + 6 * n_pages * rows * page * 4
    return pl.pallas_call(
        functools.partial(_attn_sample_kernel, layer=layer, n_pages=n_pages, dec_seq=dec_seq),
        grid_spec=grid_spec,
        out_shape=jax.ShapeDtypeStruct((batch, dec_seq, d), F32),
        compiler_params=_params(("parallel", "arbitrary"), vmem),
        name="attn_sample",
    )(page_table, q, k_new, v_new, cache_k, cache_v, blast, bnew, lam, subln[None, :])


PROMPT_TM = 512
ATTN_TILE = 512
PROJ_TN = 512
RES_TN = 1024
FFN_TF = 1024


def kernel(x_prompt, x_sample, cache_k, cache_v, page_table, rel_bias_table, mix_norm, ffn_norm,
           attn_w_qkv, attn_w_o, attn_q_norm, attn_k_norm, attn_lambda, attn_subln,
           cm_w_in, cm_v_norm, cm_w_s, cm_b_s, cm_w_o, ffn_w1, ffn_w2):
    batch, seq, d = x_prompt.shape
    dec_batch, dec_seq, _ = x_sample.shape
    depth = mix_norm.shape[0]
    n_heads = d // V_HEAD_DIM
    n_groups = cm_w_s.shape[1]
    page = cache_k.shape[2]
    mp, ms = batch * seq, dec_batch * dec_seq
    assert seq % ATTN_TILE == 0 and mp % PROMPT_TM == 0 and ms % CHUNK == 0 and CHUNK % dec_seq == 0

    xp = x_prompt.reshape(mp, d)
    xs = x_sample.reshape(ms, d)
    ck = jnp.transpose(cache_k, (0, 1, 3, 4, 5, 2)).reshape(cache_k.shape[0], cache_k.shape[1], d, page)
    cv = cache_v.reshape(cache_v.shape[0], cache_v.shape[1], page * n_heads, V_HEAD_DIM)

    bias_p = _prompt_bias(rel_bias_table, n_heads, ATTN_TILE)
    blast, bnew = _sample_bias(rel_bias_table, n_heads, dec_seq, page)

    k_prompt, v_prompt, k_sample, v_sample, cv_sample = [], [], [], [], []
    for i in range(depth):
        j = i // N_MIXERS
        if i % N_MIXERS == 0:
            w_qkv = attn_w_qkv[j].astype(BF16)
            w_o = attn_w_o[j].astype(BF16)
            qp, kp, vp, kbp, vbp = _qkv(xp, mix_norm[i], w_qkv, attn_q_norm[j], attn_k_norm[j],
                                        PROMPT_TM, PROJ_TN)
            qs, ks, vs, _, _ = _qkv(xs, mix_norm[i], w_qkv, attn_q_norm[j], attn_k_norm[j], ms, PROJ_TN)
            op = _attn_prompt(qp, kbp, vbp, bias_p, attn_lambda[j], attn_subln[j], batch, ATTN_TILE, i)
            os_ = _attn_sample(qs.astype(F32).reshape(dec_batch, dec_seq, d),
                               ks.reshape(dec_batch, dec_seq, d), vs.reshape(dec_batch, dec_seq, d),
                               ck[j], cv[j], page_table, blast, bnew, attn_lambda[j], attn_subln[j], i)
            xp = _mm_res(xp, op, w_o, 2 * PROMPT_TM, RES_TN)
            xs = _mm_res(xs, os_.reshape(ms, d), w_o, ms, RES_TN)
            k_prompt.append(kp.reshape(batch, seq, n_heads, 2, QK_HEAD_DIM))
            v_prompt.append(vp.reshape(batch, seq, n_heads, V_HEAD_DIM))
            k_sample.append(ks.reshape(dec_batch, dec_seq, n_heads, 2, QK_HEAD_DIM))
            v_sample.append(vs.reshape(dec_batch, dec_seq, n_heads, V_HEAD_DIM))
        else:
            w_in = cm_w_in[j].astype(BF16)
            w_o = cm_w_o[j].astype(BF16)
            bs_p = jnp.broadcast_to(cm_b_s[j][:, :, None], (n_groups, CHUNK, GROUP_DIM))
            gp, _ = _gmlp_front(xp, mix_norm[i], w_in, cm_v_norm[j], cm_w_s[j], bs_p,
                                PROMPT_TM, PROJ_TN, False)
            reps = CHUNK // dec_seq
            eye = jnp.eye(reps, dtype=F32)
            ws_s = (eye[None, :, None, :, None] * cm_w_s[j][:, None, :dec_seq, None, :dec_seq]
                    ).reshape(n_groups, CHUNK, CHUNK)
            bs_s = jnp.broadcast_to(jnp.tile(cm_b_s[j][:, :dec_seq], (1, reps))[:, :, None],
                                    (n_groups, CHUNK, GROUP_DIM))
            gs, cvs = _gmlp_front(xs, mix_norm[i], w_in, cm_v_norm[j], ws_s, bs_s, ms, PROJ_TN, True)
            xp = _mm_res(xp, gp, w_o, 2 * PROMPT_TM, RES_TN)
            xs = _mm_res(xs, gs, w_o, ms, RES_TN)
            cv_sample.append(cvs.reshape(dec_batch, dec_seq, n_groups, GROUP_DIM))
        w1 = ffn_w1[i].astype(BF16)
        w2 = ffn_w2[i].astype(BF16)
        xp = _ffn(xp, ffn_norm[i], w1, w2, PROMPT_TM, FFN_TF)
        xs = _ffn(xs, ffn_norm[i], w1, w2, ms, FFN_TF)

    return (xp.reshape(batch, seq, d), xs.reshape(dec_batch, dec_seq, d),
            jnp.stack(k_prompt), jnp.stack(v_prompt), jnp.stack(k_sample), jnp.stack(v_sample),
            jnp.stack(cv_sample))
```

```python
import functools
import math

import jax
import jax.numpy as jnp
from jax import lax
from jax.experimental import pallas as pl
from jax.experimental.pallas import tpu as pltpu

F32 = jnp.float32
BF16 = jnp.bfloat16

QK_HEAD_DIM = 64
V_HEAD_DIM = 128
NUM_BUCKETS = 32
MAX_DISTANCE = 128
CHUNK = 128
GROUP_DIM = 128
N_MIXERS = 2
NORM_EPS = 1e-6
SUBLN_EPS = 1e-5
NEG_INF = -1e30

LANES = 128
MXU_DIM = 256
VMEM_CAP = 56 * 1024 * 1024


def _lambda_init(layer):
    return 0.8 - 0.6 * math.exp(-0.3 * layer)


def _params(semantics, vmem_bytes):
    limit = int(min(VMEM_CAP, max(32 * 1024 * 1024, vmem_bytes * 5 // 4)))
    return pltpu.CompilerParams(dimension_semantics=semantics, vmem_limit_bytes=limit)


def _rms_rows(x, g, eps):
    ms = jnp.mean(x * x, axis=-1, keepdims=True)
    return x * lax.rsqrt(ms + eps) * g


def _group_rms(x, bsum, group, eps):
    x2 = x * x
    hi = x2.astype(BF16)
    lo = (x2 - hi.astype(F32)).astype(BF16)
    parts = []
    for c in range(x.shape[1] // MXU_DIM):
        sl = slice(c * MXU_DIM, (c + 1) * MXU_DIM)
        parts.append(jnp.dot(hi[:, sl], bsum, preferred_element_type=F32)
                     + jnp.dot(lo[:, sl], bsum, preferred_element_type=F32))
    ss = parts[0] if len(parts) == 1 else jnp.concatenate(parts, axis=1)
    return x * lax.rsqrt(ss * (1.0 / group) + eps)


def _group_sum_matrix(group):
    r = jnp.arange(MXU_DIM) // group
    return (r[:, None] == r[None, :]).astype(BF16)


def _diff_lambda_value(lam_ref, layer):
    lf = lam_ref[...]
    a = jnp.sum(lf[0:1] * lf[1:2], axis=-1, keepdims=True)
    b = jnp.sum(lf[2:3] * lf[3:4], axis=-1, keepdims=True)
    return jnp.exp(a) - jnp.exp(b) + _lambda_init(layer)


def _qkv_kernel(x_ref, g_ref, wq_ref, wk_ref, wv_ref, qg_ref, kg_ref, bsum_ref, *rest, feature_major):
    if feature_major:
        qt_ref, kb_ref, kt_ref, v_ref, vt_ref, h_scr = rest
    else:
        q_ref, k_ref, v_ref, h_scr = rest

    @pl.when(pl.program_id(1) == 0)
    def _():
        h_scr[...] = _rms_rows(x_ref[...], g_ref[...], NORM_EPS).astype(BF16)

    h = h_scr[...]
    bsum = bsum_ref[...]
    q = jnp.dot(h, wq_ref[...], preferred_element_type=F32)
    qn = _group_rms(q, bsum, QK_HEAD_DIM, NORM_EPS) * qg_ref[...] * (QK_HEAD_DIM ** -0.5)
    k = jnp.dot(h, wk_ref[...], preferred_element_type=F32)
    kn = _group_rms(k, bsum, QK_HEAD_DIM, NORM_EPS) * kg_ref[...]
    v = jnp.dot(h, wv_ref[...], preferred_element_type=F32)
    v_ref[...] = v
    if feature_major:
        qt_ref[0] = qn.T.astype(BF16)
        kb_ref[...] = kn.astype(BF16)
        kt_ref[0] = kn.T
        vt_ref[0] = v.T.astype(BF16)
    else:
        q_ref[...] = qn
        k_ref[...] = kn


def _qkv(x, g, w_qkv, q_gain, k_gain, tm, tn, seq=None):
    m, d = x.shape
    n = w_qkv.shape[1] // 3
    nj = n // tn
    qg = jnp.tile(q_gain, tn // QK_HEAD_DIM)[None, :]
    kg = jnp.tile(k_gain, tn // QK_HEAD_DIM)[None, :]
    bsum = _group_sum_matrix(QK_HEAD_DIM)
    row = pl.BlockSpec((tm, d), lambda i, j: (i, 0))
    col = lambda off: pl.BlockSpec((d, tn), lambda i, j, off=off: (0, j + off))
    vec = lambda width: pl.BlockSpec((1, width), lambda i, j: (0, 0))
    out = pl.BlockSpec((tm, tn), lambda i, j: (i, j))
    if seq is None:
        out_specs = [out, out, out]
        out_shape = [jax.ShapeDtypeStruct((m, n), F32)] * 3
    else:
        per_seq = seq // tm
        tile_t = pl.BlockSpec((1, tn, tm), lambda i, j: (i, j, 0))
        seq_t = pl.BlockSpec((1, tn, tm), lambda i, j: (i // per_seq, j, i % per_seq))
        out_specs = [tile_t, out, seq_t, out, tile_t]
        out_shape = [jax.ShapeDtypeStruct((m // tm, n, tm), BF16), jax.ShapeDtypeStruct((m, n), BF16),
                     jax.ShapeDtypeStruct((m // seq, n, seq), F32), jax.ShapeDtypeStruct((m, n), F32),
                     jax.ShapeDtypeStruct((m // tm, n, tm), BF16)]
    vmem = 2 * tm * d * 4 + tm * d * 2 + 6 * d * tn * 2 + 2 * tm * tn * 16 + 10 * tm * tn * 4
    return pl.pallas_call(
        functools.partial(_qkv_kernel, feature_major=seq is not None),
        grid=(m // tm, nj),
        in_specs=[row, vec(d), col(0), col(nj), col(2 * nj), vec(tn), vec(tn),
                  pl.BlockSpec((MXU_DIM, MXU_DIM), lambda i, j: (0, 0))],
        out_specs=out_specs,
        out_shape=out_shape,
        scratch_shapes=[pltpu.VMEM((tm, d), BF16)],
        compiler_params=_params(("parallel", "arbitrary"), vmem),
        name="qkv",
    )(x, g[None, :], w_qkv, w_qkv, w_qkv, qg, kg, bsum)


def _mm_res_kernel(x_ref, a_ref, w_ref, o_ref):
    o_ref[...] = x_ref[...] + jnp.dot(a_ref[...].astype(BF16), w_ref[...],
                                      preferred_element_type=F32)


def _mm_res(x, a, w, tm, tn):
    m, n = x.shape
    kd = a.shape[1]
    vmem = 2 * tm * kd * a.dtype.itemsize + 2 * kd * tn * 2 + 4 * tm * tn * 4 + tm * kd * 2 + tm * tn * 4
    return pl.pallas_call(
        _mm_res_kernel,
        grid=(m // tm, n // tn),
        in_specs=[pl.BlockSpec((tm, tn), lambda i, j: (i, j)),
                  pl.BlockSpec((tm, kd), lambda i, j: (i, 0)),
                  pl.BlockSpec((kd, tn), lambda i, j: (0, j))],
        out_specs=pl.BlockSpec((tm, tn), lambda i, j: (i, j)),
        out_shape=jax.ShapeDtypeStruct((m, n), F32),
        compiler_params=_params(("parallel", "arbitrary"), vmem),
        name="mm_res",
    )(x, a, w)


def _ffn_kernel(x_ref, g_ref, w1_ref, w2_ref, o_ref, h_scr):
    @pl.when(pl.program_id(1) == 0)
    def _():
        x = x_ref[...]
        h_scr[...] = _rms_rows(x, g_ref[...], NORM_EPS).astype(BF16)
        o_ref[...] = x

    a = jnp.dot(h_scr[...], w1_ref[...], preferred_element_type=F32)
    a = jnp.maximum(a, 0.0)
    a = (a * a).astype(BF16)
    o_ref[...] += jnp.dot(a, w2_ref[...], preferred_element_type=F32)


def _ffn(x, g, w1, w2, layer, tm, tf):
    m, d = x.shape
    dff = w1.shape[2]
    vmem = 4 * tm * d * 4 + tm * d * 2 + 4 * d * tf * 2 + tm * tf * 8
    return pl.pallas_call(
        _ffn_kernel,
        grid=(m // tm, dff // tf),
        in_specs=[pl.BlockSpec((tm, d), lambda i, f: (i, 0)),
                  pl.BlockSpec((1, d), lambda i, f: (0, 0)),
                  pl.BlockSpec((None, d, tf), lambda i, f: (layer, 0, f)),
                  pl.BlockSpec((None, tf, d), lambda i, f: (layer, f, 0))],
        out_specs=pl.BlockSpec((tm, d), lambda i, f: (i, 0)),
        out_shape=jax.ShapeDtypeStruct((m, d), F32),
        scratch_shapes=[pltpu.VMEM((tm, d), BF16)],
        compiler_params=_params(("parallel", "arbitrary"), vmem),
        name="ffn",
    )(x, g[None, :], w1, w2)


def _gelu(x):
    return 0.5 * x * (1.0 + lax.erf(x * (2.0 ** -0.5)))


def _gmlp_kernel(x_ref, g_ref, wu_ref, wv_ref, vg_ref, bsum_ref, ws_ref, bs_ref, *rest, emit_cv):
    if emit_cv:
        gate_ref, cv_ref, h_scr = rest
    else:
        gate_ref, h_scr = rest

    @pl.when(pl.program_id(1) == 0)
    def _():
        h_scr[...] = _rms_rows(x_ref[...], g_ref[...], NORM_EPS).astype(BF16)

    h = h_scr[...]
    tm = h.shape[0]
    tn = wu_ref.shape[1]
    u = _gelu(jnp.dot(h, wu_ref[...], preferred_element_type=F32))
    v = _gelu(jnp.dot(h, wv_ref[...], preferred_element_type=F32))
    vn = _group_rms(v, bsum_ref[...], GROUP_DIM, NORM_EPS) * vg_ref[...]
    if emit_cv:
        cv_ref[...] = vn
    vb = vn.astype(BF16)
    t_idx = lax.broadcasted_iota(jnp.int32, (CHUNK, CHUNK), 0)
    s_idx = lax.broadcasted_iota(jnp.int32, (CHUNK, CHUNK), 1)
    causal = t_idx >= s_idx
    for gi in range(tn // GROUP_DIM):
        cols = slice(gi * GROUP_DIM, (gi + 1) * GROUP_DIM)
        w = jnp.where(causal, ws_ref[gi], 0.0).astype(BF16)
        bias = bs_ref[gi]
        for c in range(tm // CHUNK):
            rows = slice(c * CHUNK, (c + 1) * CHUNK)
            mixed = jnp.dot(w, vb[rows, cols], preferred_element_type=F32) + bias
            gate_ref[rows, cols] = (u[rows, cols] * mixed).astype(BF16)


def _gmlp_front(x, g, w_in, v_gain, ws, bs, tm, tn, emit_cv):
    m, d = x.shape
    dg = w_in.shape[1] // 2
    nj = dg // tn
    gpt = tn // GROUP_DIM
    blk = pl.BlockSpec((tm, tn), lambda i, j: (i, j))
    out_specs = [blk, blk] if emit_cv else [blk]
    out_shape = [jax.ShapeDtypeStruct((m, dg), BF16)]
    if emit_cv:
        out_shape.append(jax.ShapeDtypeStruct((m, dg), F32))
    vmem = 2 * tm * d * 4 + tm * d * 2 + 4 * d * tn * 2 + 4 * gpt * CHUNK * CHUNK * 4 + 12 * tm * tn * 4
    res = pl.pallas_call(
        functools.partial(_gmlp_kernel, emit_cv=emit_cv),
        grid=(m // tm, nj),
        in_specs=[pl.BlockSpec((tm, d), lambda i, j: (i, 0)),
                  pl.BlockSpec((1, d), lambda i, j: (0, 0)),
                  pl.BlockSpec((d, tn), lambda i, j: (0, j)),
                  pl.BlockSpec((d, tn), lambda i, j, nj=nj: (0, j + nj)),
                  pl.BlockSpec((1, tn), lambda i, j: (0, j)),
                  pl.BlockSpec((MXU_DIM, MXU_DIM), lambda i, j: (0, 0)),
                  pl.BlockSpec((gpt, CHUNK, CHUNK), lambda i, j: (j, 0, 0)),
                  pl.BlockSpec((gpt, CHUNK, GROUP_DIM), lambda i, j: (j, 0, 0))],
        out_specs=out_specs,
        out_shape=out_shape,
        scratch_shapes=[pltpu.VMEM((tm, d), BF16)],
        compiler_params=_params(("parallel", "arbitrary"), vmem),
        name="gmlp_front",
    )(x, g[None, :], w_in, w_in, v_gain[None, :], _group_sum_matrix(GROUP_DIM), ws, bs)
    return res if emit_cv else (res[0], None)


def _shifted_bias(dist, table_ref, head):
    max_exact = NUM_BUCKETS // 2
    n = jnp.maximum(dist, 0)
    nf = jnp.maximum(n, 1).astype(F32)
    large = max_exact + (jnp.log(nf / max_exact) / math.log(MAX_DISTANCE / max_exact)
                         * (NUM_BUCKETS - max_exact)).astype(jnp.int32)
    bucket = jnp.where(n < max_exact, n, jnp.minimum(large, NUM_BUCKETS - 1))
    far = table_ref[NUM_BUCKETS - 1, head]
    acc = jnp.zeros(dist.shape, F32)
    for b in range(NUM_BUCKETS - 1):
        acc = jnp.where(bucket == b, table_ref[b, head] - far, acc)
    return acc


def _prompt_bias_kernel(table_ref, o_ref, *, tile):
    head = pl.program_id(0)
    key = lax.broadcasted_iota(jnp.int32, (LANES, LANES), 0)
    qry = lax.broadcasted_iota(jnp.int32, (LANES, LANES), 1)
    d0 = jnp.where(qry >= key, _shifted_bias(qry - key, table_ref, head), NEG_INF)
    d1 = _shifted_bias(LANES + qry - key, table_ref, head)
    nb = tile // LANES
    zeros = jnp.zeros((LANES, LANES), F32)
    masked = jnp.full((LANES, LANES), NEG_INF, F32)
    for a in range(nb):
        for b in range(nb):
            rows = slice(a * LANES, (a + 1) * LANES)
            cols = slice(b * LANES, (b + 1) * LANES)
            diag = d0 if b == a else d1 if b == a + 1 else zeros if b > a else masked
            o_ref[0, 0, rows, cols] = diag
            o_ref[0, 1, rows, cols] = d1 if (a == nb - 1 and b == 0) else zeros


def _prompt_bias(table, n_heads, tile):
    return pl.pallas_call(
        functools.partial(_prompt_bias_kernel, tile=tile),
        grid=(n_heads,),
        in_specs=[pl.BlockSpec(memory_space=pltpu.SMEM)],
        out_specs=pl.BlockSpec((1, 2, tile, tile), lambda h: (h, 0, 0, 0)),
        out_shape=jax.ShapeDtypeStruct((n_heads, 2, tile, tile), F32),
        compiler_params=_params(("parallel",), 8 * tile * tile * 4),
        name="prompt_bias",
    )(table)


def _sample_bias_kernel(table_ref, last_ref, new_ref, *, n_heads, dec_seq, page):
    rows = 2 * n_heads * dec_seq
    ridx = lax.broadcasted_iota(jnp.int32, (rows, page), 0)
    key = lax.broadcasted_iota(jnp.int32, (rows, page), 1)
    tok = ridx % dec_seq
    row_head = (ridx % (n_heads * dec_seq)) // dec_seq
    last = jnp.zeros((rows, page), F32)
    new = jnp.zeros((rows, page), F32)
    for h in range(n_heads):
        last = jnp.where(row_head == h, _shifted_bias(page + tok - key, table_ref, h), last)
        new = jnp.where(row_head == h, _shifted_bias(tok - key, table_ref, h), new)
    last_ref[...] = last
    new_ref[...] = jnp.where(key <= tok, new, NEG_INF)


def _sample_bias(table, n_heads, dec_seq, page):
    rows = 2 * n_heads * dec_seq
    return pl.pallas_call(
        functools.partial(_sample_bias_kernel, n_heads=n_heads, dec_seq=dec_seq, page=page),
        in_specs=[pl.BlockSpec(memory_space=pltpu.SMEM)],
        out_specs=[pl.BlockSpec(memory_space=pltpu.VMEM), pl.BlockSpec(memory_space=pltpu.VMEM)],
        out_shape=[jax.ShapeDtypeStruct((rows, page), F32), jax.ShapeDtypeStruct((rows, page), F32)],
        name="sample_bias",
    )(table)


def _attn_prompt_kernel(qt_ref, k_ref, vt_ref, bias_ref, lam_ref, sg_ref, o_ref,
                        qs_scr, m_scr, l_scr, acc_scr, *, layer):
    i = pl.program_id(2)
    t = qt_ref.shape[2]
    hd = QK_HEAD_DIM
    qt = qt_ref[0]
    zero = jnp.zeros((hd, t), BF16)
    qs_scr[0:hd, 0:t] = qt[0:hd]
    qs_scr[hd:2 * hd, 0:t] = zero
    qs_scr[0:hd, t:2 * t] = zero
    qs_scr[hd:2 * hd, t:2 * t] = qt[hd:2 * hd]
    m_scr[...] = jnp.full(m_scr.shape, NEG_INF, F32)
    l_scr[...] = jnp.zeros(l_scr.shape, F32)
    acc_scr[...] = jnp.zeros(acc_scr.shape, F32)

    def block(j, bias):
        k = k_ref[pl.ds(pl.multiple_of(j * t, t), t), :]
        s = jnp.dot(k, qs_scr[...], preferred_element_type=F32)
        if bias is not None:
            s = s + jnp.concatenate([bias, bias], axis=1)
        m_prev = m_scr[...]
        m_new = jnp.maximum(m_prev, jnp.max(s, axis=0, keepdims=True))
        alpha = jnp.exp(m_prev - m_new)
        p = jnp.exp(s - m_new)
        l_scr[...] = alpha * l_scr[...] + jnp.sum(p, axis=0, keepdims=True)
        acc_scr[...] = alpha * acc_scr[...] + jnp.dot(vt_ref[j], p.astype(BF16),
                                                      preferred_element_type=F32)
        m_scr[...] = m_new

    def far_block(j, carry):
        block(j, None)
        return carry

    lax.fori_loop(0, i - 1, far_block, 0)

    @pl.when(i >= 1)
    def _():
        block(i - 1, bias_ref[0, 1])

    block(i, bias_ref[0, 0])
    lam = _diff_lambda_value(lam_ref, layer)
    o = acc_scr[...] / l_scr[...]
    od = o[:, 0:t] - lam * o[:, t:2 * t]
    on = od * lax.rsqrt(jnp.mean(od * od, axis=0, keepdims=True) + SUBLN_EPS)
    o_ref[...] = (on.T * sg_ref[...] * (1.0 - _lambda_init(layer))).astype(BF16)


def _attn_prompt(qt, kb, vt, bias, lam, subln, batch, layer):
    n_tiles, d, tile = qt.shape
    m = kb.shape[0]
    seq = m // batch
    nq = seq // tile
    n_heads = d // V_HEAD_DIM
    vmem = (4 * seq * V_HEAD_DIM * 2 + 4 * tile * tile * 4 + 5 * tile * 2 * tile * 4
            + 4 * V_HEAD_DIM * 2 * tile * 4)
    return pl.pallas_call(
        functools.partial(_attn_prompt_kernel, layer=layer),
        grid=(batch, n_heads, nq),
        in_specs=[pl.BlockSpec((1, V_HEAD_DIM, tile), lambda b, h, i: (b * nq + i, h, 0)),
                  pl.BlockSpec((seq, V_HEAD_DIM), lambda b, h, i: (b, h)),
                  pl.BlockSpec((nq, V_HEAD_DIM, tile), lambda b, h, i: (b, h, 0)),
                  pl.BlockSpec((1, 2, tile, tile), lambda b, h, i: (h, 0, 0, 0)),
                  pl.BlockSpec(lam.shape, lambda b, h, i: (0, 0)),
                  pl.BlockSpec((1, V_HEAD_DIM), lambda b, h, i: (0, 0))],
        out_specs=pl.BlockSpec((tile, V_HEAD_DIM), lambda b, h, i: (b * nq + i, h)),
        out_shape=jax.ShapeDtypeStruct((m, d), BF16),
        scratch_shapes=[pltpu.VMEM((V_HEAD_DIM, 2 * tile), BF16),
                        pltpu.VMEM((1, 2 * tile), F32),
                        pltpu.VMEM((1, 2 * tile), F32),
                        pltpu.VMEM((V_HEAD_DIM, 2 * tile), F32)],
        compiler_params=_params(("parallel", "parallel", "arbitrary"), vmem),
        name="attn_prompt",
    )(qt, kb, vt, bias, lam, subln[None, :])


def _attn_sample_kernel(pt_ref, q_ref, kn_ref, vn_ref, *rest, layer, n_pages, dec_seq, slots):
    kc_refs, vc_refs = rest[:slots], rest[slots:2 * slots]
    (blast_ref, bnew_ref, lam_ref, sg_ref, o_ref,
     qbd_scr, lg_scr, w_scr, wn_scr, acc_scr) = rest[2 * slots:]
    s = pl.program_id(1)
    k_steps = n_pages // slots
    d = q_ref.shape[2]
    n_heads = d // V_HEAD_DIM
    rows = 2 * n_heads * dec_seq
    half = rows // 2
    page = kc_refs[0].shape[1]
    group_rows = acc_scr.shape[1]
    group_heads = group_rows // dec_seq

    def padded_new(ref):
        x = ref[0]
        return jnp.concatenate([x, jnp.zeros((page - x.shape[0], d), F32)], axis=0).astype(BF16)

    @pl.when(s == 0)
    def _():
        q = q_ref[0]
        qrep = jnp.broadcast_to(q[None], (rows // q.shape[0], q.shape[0], d)).reshape(rows, d)
        ridx = lax.broadcasted_iota(jnp.int32, (rows, d), 0)
        lane = lax.broadcasted_iota(jnp.int32, (rows, d), 1)
        row_head = (ridx % half) // dec_seq
        row_map = ridx // half
        match = (lane // V_HEAD_DIM == row_head) & ((lane // QK_HEAD_DIM) % 2 == row_map)
        qbd_scr[...] = jnp.where(match, qrep, 0.0).astype(BF16)
        acc_scr[...] = jnp.zeros(acc_scr.shape, F32)

    @pl.when(s < k_steps)
    def _():
        for e in range(slots):
            lg_scr[s * slots + e] = jnp.dot(qbd_scr[...], kc_refs[e][...].astype(BF16),
                                            preferred_element_type=F32)

    @pl.when(s == k_steps - 1)
    def _():
        lam = _diff_lambda_value(lam_ref, layer)
        lg_scr[n_pages - 1] = lg_scr[n_pages - 1] + blast_ref[...]
        lgn = lax.dot_general(qbd_scr[...], padded_new(kn_ref), (((1,), (1,)), ((), ())),
                              preferred_element_type=F32) + bnew_ref[...]
        lg = lg_scr[...]
        m = jnp.maximum(jnp.max(jnp.max(lg, axis=0), axis=-1, keepdims=True),
                        jnp.max(lgn, axis=-1, keepdims=True))
        p = jnp.exp(lg - m[None])
        pn = jnp.exp(lgn - m)
        l = jnp.sum(jnp.sum(p, axis=0), axis=-1, keepdims=True) + jnp.sum(pn, axis=-1, keepdims=True)
        inv = 1.0 / l
        p = p * inv[None]
        pn = pn * inv
        w_scr[...] = p[:, 0:half, :] - lam * p[:, half:rows, :]
        wn_scr[...] = (pn[0:half, :] - lam * pn[half:rows, :]).astype(BF16)

    @pl.when(s >= k_steps)
    def _():
        for g in range(n_heads // group_heads):
            acc = acc_scr[g]
            for e in range(slots):
                v = jnp.concatenate(
                    [vc_refs[e][pl.ds(g * group_heads + hh, page, stride=n_heads), :].astype(BF16)
                     for hh in range(group_heads)], axis=1)
                w = w_scr[(s - k_steps) * slots + e, g * group_rows:(g + 1) * group_rows, :]
                acc = acc + jnp.dot(w.astype(BF16), v, preferred_element_type=F32)
            acc_scr[g] = acc

    @pl.when(s == 2 * k_steps - 1)
    def _():
        new = jnp.dot(wn_scr[...], padded_new(vn_ref), preferred_element_type=F32)
        scale = 1.0 - _lambda_init(layer)
        for h in range(n_heads):
            g, e = divmod(h, group_heads)
            cols = slice(h * V_HEAD_DIM, (h + 1) * V_HEAD_DIM)
            piece = (acc_scr[g, e * dec_seq:(e + 1) * dec_seq, e * V_HEAD_DIM:(e + 1) * V_HEAD_DIM]
                     + new[h * dec_seq:(h + 1) * dec_seq, cols])
            o_ref[0, :, cols] = _rms_rows(piece, sg_ref[...], SUBLN_EPS) * scale


def _attn_sample(q, k_new, v_new, cache_k, cache_v, page_table, blast, bnew, lam, subln, layer):
    batch, dec_seq, d = q.shape
    n_pages = page_table.shape[1]
    page = cache_k.shape[2]
    n_heads = d // V_HEAD_DIM
    rows = 2 * n_heads * dec_seq
    sub_f32, sub_bf16 = 8, 16
    assert sub_f32 % dec_seq == 0
    q = jnp.tile(q, (1, sub_f32 // dec_seq, 1))
    k_new = jnp.pad(k_new, ((0, 0), (0, sub_bf16 - dec_seq), (0, 0)))
    v_new = jnp.pad(v_new, ((0, 0), (0, sub_bf16 - dec_seq), (0, 0)))
    tok = lambda n: pl.BlockSpec((1, n, d), lambda b, s, pt: (b, 0, 0))
    full = lambda a: pl.BlockSpec(a.shape, lambda b, s, pt: (0,) * a.ndim)
    slots = SAMPLE_PAGE_SLOTS
    assert n_pages % slots == 0
    k_steps = n_pages // slots

    def key_page(e):
        return lambda b, s, pt: (pt[b, jnp.minimum(s, k_steps - 1) * slots + e], 0, 0)

    def value_page(e):
        def index(b, s, pt):
            weighting = s >= k_steps
            row = jnp.where(weighting, b, jnp.maximum(b - 1, 0))
            col = jnp.where(weighting, s - k_steps, k_steps - 1) * slots + e
            return (pt[row, col], 0, 0)
        return index

    grid_spec = pltpu.PrefetchScalarGridSpec(
        num_scalar_prefetch=1,
        grid=(batch, 2 * k_steps),
        in_specs=[tok(sub_f32), tok(sub_bf16), tok(sub_bf16)]
                 + [pl.BlockSpec((None, d, page), key_page(e)) for e in range(slots)]
                 + [pl.BlockSpec((None, page * n_heads, V_HEAD_DIM), value_page(e)) for e in range(slots)]
                 + [full(blast), full(bnew), full(lam),
                    pl.BlockSpec((1, V_HEAD_DIM), lambda b, s, pt: (0, 0))],
        out_specs=pl.BlockSpec((1, dec_seq, d), lambda b, s, pt: (b, 0, 0)),
        scratch_shapes=[pltpu.VMEM((rows, d), BF16),
                        pltpu.VMEM((n_pages, rows, page), F32),
                        pltpu.VMEM((n_pages, rows // 2, page), F32),
                        pltpu.VMEM((rows // 2, page), BF16),
                        pltpu.VMEM((n_heads * dec_seq // sub_f32, sub_f32,
                                    sub_f32 // dec_seq * V_HEAD_DIM), F32)],
    )
    vmem = 4 * slots * page * d * 4 + 6 * n_pages * rows * page * 4 + 4 * page * d * 2
    return pl.pallas_call(
        functools.partial(_attn_sample_kernel, layer=layer, n_pages=n_pages, dec_seq=dec_seq, slots=slots),
        grid_spec=grid_spec,
        out_shape=jax.ShapeDtypeStruct((batch, dec_seq, d), F32),
        compiler_params=_params(("arbitrary", "arbitrary"), vmem),
        name="attn_sample",
    )(page_table, q, k_new, v_new, *([cache_k] * slots), *([cache_v] * slots),
      blast, bnew, lam, subln[None, :])


PROMPT_TM = 512
ATTN_TILE = 512
PROJ_TN = 512
RES_TN = 1024
FFN_TF = 1024
SAMPLE_PAGE_SLOTS = 8


def kernel(x_prompt, x_sample, cache_k, cache_v, page_table, rel_bias_table, mix_norm, ffn_norm,
           attn_w_qkv, attn_w_o, attn_q_norm, attn_k_norm, attn_lambda, attn_subln,
           cm_w_in, cm_v_norm, cm_w_s, cm_b_s, cm_w_o, ffn_w1, ffn_w2):
    batch, seq, d = x_prompt.shape
    dec_batch, dec_seq, _ = x_sample.shape
    depth = mix_norm.shape[0]
    n_heads = d // V_HEAD_DIM
    n_groups = cm_w_s.shape[1]
    page = cache_k.shape[2]
    mp, ms = batch * seq, dec_batch * dec_seq
    assert seq % ATTN_TILE == 0 and mp % PROMPT_TM == 0 and ms % CHUNK == 0 and CHUNK % dec_seq == 0

    xp = x_prompt.reshape(mp, d)
    xs = x_sample.reshape(ms, d)
    ck = jnp.transpose(cache_k, (0, 1, 3, 4, 5, 2)).reshape(cache_k.shape[0], cache_k.shape[1], d, page)
    cv = cache_v.reshape(cache_v.shape[0], cache_v.shape[1], page * n_heads, V_HEAD_DIM)

    bias_p = _prompt_bias(rel_bias_table, n_heads, ATTN_TILE)
    blast, bnew = _sample_bias(rel_bias_table, n_heads, dec_seq, page)

    w1_all = ffn_w1.astype(BF16)
    w2_all = ffn_w2.astype(BF16)
    k_prompt, v_prompt, k_sample, v_sample, cv_sample = [], [], [], [], []
    for i in range(depth):
        j = i // N_MIXERS
        if i % N_MIXERS == 0:
            w_qkv = attn_w_qkv[j].astype(BF16)
            w_o = attn_w_o[j].astype(BF16)
            qtp, kbp, ktp, vp, vtp = _qkv(xp, mix_norm[i], w_qkv, attn_q_norm[j], attn_k_norm[j],
                                          ATTN_TILE, PROJ_TN, seq=seq)
            qs, ks, vs = _qkv(xs, mix_norm[i], w_qkv, attn_q_norm[j], attn_k_norm[j], ms, PROJ_TN)
            op = _attn_prompt(qtp, kbp, vtp, bias_p, attn_lambda[j], attn_subln[j], batch, i)
            os_ = _attn_sample(qs.reshape(dec_batch, dec_seq, d),
                               ks.reshape(dec_batch, dec_seq, d), vs.reshape(dec_batch, dec_seq, d),
                               ck[j], cv[j], page_table, blast, bnew, attn_lambda[j], attn_subln[j], i)
            xp = _mm_res(xp, op, w_o, 2 * PROMPT_TM, RES_TN)
            xs = _mm_res(xs, os_.reshape(ms, d), w_o, ms, RES_TN)
            k_prompt.append(jnp.transpose(ktp.reshape(batch, n_heads, 2, QK_HEAD_DIM, seq), (0, 4, 1, 2, 3)))
            v_prompt.append(vp.reshape(batch, seq, n_heads, V_HEAD_DIM))
            k_sample.append(ks.reshape(dec_batch, dec_seq, n_heads, 2, QK_HEAD_DIM))
            v_sample.append(vs.reshape(dec_batch, dec_seq, n_heads, V_HEAD_DIM))
        else:
            w_in = cm_w_in[j].astype(BF16)
            w_o = cm_w_o[j].astype(BF16)
            bs_p = jnp.broadcast_to(cm_b_s[j][:, :, None], (n_groups, CHUNK, GROUP_DIM))
            gp, _ = _gmlp_front(xp, mix_norm[i], w_in, cm_v_norm[j], cm_w_s[j], bs_p,
                                PROMPT_TM, PROJ_TN, False)
            reps = CHUNK // dec_seq
            eye = jnp.eye(reps, dtype=F32)
            ws_s = (eye[None, :, None, :, None] * cm_w_s[j][:, None, :dec_seq, None, :dec_seq]
                    ).reshape(n_groups, CHUNK, CHUNK)
            bs_s = jnp.broadcast_to(jnp.tile(cm_b_s[j][:, :dec_seq], (1, reps))[:, :, None],
                                    (n_groups, CHUNK, GROUP_DIM))
            gs, cvs = _gmlp_front(xs, mix_norm[i], w_in, cm_v_norm[j], ws_s, bs_s, ms, PROJ_TN, True)
            xp = _mm_res(xp, gp, w_o, 2 * PROMPT_TM, RES_TN)
            xs = _mm_res(xs, gs, w_o, ms, RES_TN)
            cv_sample.append(cvs.reshape(dec_batch, dec_seq, n_groups, GROUP_DIM))
        xp = _ffn(xp, ffn_norm[i], w1_all, w2_all, i, PROMPT_TM, FFN_TF)
        xs = _ffn(xs, ffn_norm[i], w1_all, w2_all, i, ms, FFN_TF)

    return (xp.reshape(batch, seq, d), xs.reshape(dec_batch, dec_seq, d),
            jnp.stack(k_prompt), jnp.stack(v_prompt), jnp.stack(k_sample), jnp.stack(v_sample),
            jnp.stack(cv_sample))
```

```python
import functools
import math

import jax
import jax.numpy as jnp
from jax import lax
from jax.experimental import pallas as pl
from jax.experimental.pallas import tpu as pltpu

F32 = jnp.float32
BF16 = jnp.bfloat16

QK_HEAD_DIM = 64
V_HEAD_DIM = 128
NUM_BUCKETS = 32
MAX_DISTANCE = 128
CHUNK = 128
GROUP_DIM = 128
N_MIXERS = 2
NORM_EPS = 1e-6
SUBLN_EPS = 1e-5
NEG_INF = -1e30

LANES = 128
VALUE_TILE_HEADS = 8
MXU_DIM = 256
VMEM_CAP = 56 * 1024 * 1024


def _lambda_init(layer):
    return 0.8 - 0.6 * math.exp(-0.3 * layer)


def _params(semantics, vmem_bytes):
    limit = int(min(VMEM_CAP, max(32 * 1024 * 1024, vmem_bytes * 5 // 4)))
    return pltpu.CompilerParams(dimension_semantics=semantics, vmem_limit_bytes=limit)


def _rms_rows(x, g, eps):
    ms = jnp.mean(x * x, axis=-1, keepdims=True)
    return x * lax.rsqrt(ms + eps) * g


def _group_rms(x, bsum, group, eps):
    x2 = x * x
    hi = x2.astype(BF16)
    lo = (x2 - hi.astype(F32)).astype(BF16)
    parts = []
    for c in range(x.shape[1] // MXU_DIM):
        sl = slice(c * MXU_DIM, (c + 1) * MXU_DIM)
        parts.append(jnp.dot(hi[:, sl], bsum, preferred_element_type=F32)
                     + jnp.dot(lo[:, sl], bsum, preferred_element_type=F32))
    ss = parts[0] if len(parts) == 1 else jnp.concatenate(parts, axis=1)
    return x * lax.rsqrt(ss * (1.0 / group) + eps)


def _group_sum_matrix(group):
    r = jnp.arange(MXU_DIM) // group
    return (r[:, None] == r[None, :]).astype(BF16)


def _diff_lambda_value(lam_ref, layer):
    lf = lam_ref[...]
    a = jnp.sum(lf[0:1] * lf[1:2], axis=-1, keepdims=True)
    b = jnp.sum(lf[2:3] * lf[3:4], axis=-1, keepdims=True)
    return jnp.exp(a) - jnp.exp(b) + _lambda_init(layer)


def _qkv_kernel(x_ref, g_ref, wq_ref, wk_ref, wv_ref, qg_ref, kg_ref, bsum_ref, *rest, feature_major):
    if feature_major:
        qt_ref, kb_ref, kt_ref, v_ref, vt_ref, h_scr = rest
    else:
        q_ref, k_ref, v_ref, h_scr = rest

    @pl.when(pl.program_id(1) == 0)
    def _():
        h_scr[...] = _rms_rows(x_ref[...], g_ref[...], NORM_EPS).astype(BF16)

    h = h_scr[...]
    bsum = bsum_ref[...]
    q = jnp.dot(h, wq_ref[...], preferred_element_type=F32)
    qn = _group_rms(q, bsum, QK_HEAD_DIM, NORM_EPS) * qg_ref[...] * (QK_HEAD_DIM ** -0.5)
    k = jnp.dot(h, wk_ref[...], preferred_element_type=F32)
    kn = _group_rms(k, bsum, QK_HEAD_DIM, NORM_EPS) * kg_ref[...]
    v = jnp.dot(h, wv_ref[...], preferred_element_type=F32)
    v_ref[...] = v
    if feature_major:
        qt_ref[0] = qn.T.astype(BF16)
        kb_ref[...] = kn.astype(BF16)
        kt_ref[0] = kn.T
        vt_ref[0] = v.T.astype(BF16)
    else:
        q_ref[...] = qn
        k_ref[...] = kn


def _qkv(x, g, w_qkv, q_gain, k_gain, tm, tn, seq=None):
    m, d = x.shape
    n = w_qkv.shape[1] // 3
    nj = n // tn
    qg = jnp.tile(q_gain, tn // QK_HEAD_DIM)[None, :]
    kg = jnp.tile(k_gain, tn // QK_HEAD_DIM)[None, :]
    bsum = _group_sum_matrix(QK_HEAD_DIM)
    row = pl.BlockSpec((tm, d), lambda i, j: (i, 0))
    col = lambda off: pl.BlockSpec((d, tn), lambda i, j, off=off: (0, j + off))
    vec = lambda width: pl.BlockSpec((1, width), lambda i, j: (0, 0))
    out = pl.BlockSpec((tm, tn), lambda i, j: (i, j))
    if seq is None:
        out_specs = [out, out, out]
        out_shape = [jax.ShapeDtypeStruct((m, n), F32)] * 3
    else:
        per_seq = seq // tm
        tile_t = pl.BlockSpec((1, tn, tm), lambda i, j: (i, j, 0))
        seq_t = pl.BlockSpec((1, tn, tm), lambda i, j: (i // per_seq, j, i % per_seq))
        out_specs = [tile_t, out, seq_t, out, tile_t]
        out_shape = [jax.ShapeDtypeStruct((m // tm, n, tm), BF16), jax.ShapeDtypeStruct((m, n), BF16),
                     jax.ShapeDtypeStruct((m // seq, n, seq), F32), jax.ShapeDtypeStruct((m, n), F32),
                     jax.ShapeDtypeStruct((m // tm, n, tm), BF16)]
    vmem = 2 * tm * d * 4 + tm * d * 2 + 6 * d * tn * 2 + 2 * tm * tn * 16 + 10 * tm * tn * 4
    return pl.pallas_call(
        functools.partial(_qkv_kernel, feature_major=seq is not None),
        grid=(m // tm, nj),
        in_specs=[row, vec(d), col(0), col(nj), col(2 * nj), vec(tn), vec(tn),
                  pl.BlockSpec((MXU_DIM, MXU_DIM), lambda i, j: (0, 0))],
        out_specs=out_specs,
        out_shape=out_shape,
        scratch_shapes=[pltpu.VMEM((tm, d), BF16)],
        compiler_params=_params(("parallel", "arbitrary"), vmem),
        name="qkv",
    )(x, g[None, :], w_qkv, w_qkv, w_qkv, qg, kg, bsum)


def _mm_res_kernel(x_ref, a_ref, w_ref, o_ref):
    o_ref[...] = x_ref[...] + jnp.dot(a_ref[...].astype(BF16), w_ref[...],
                                      preferred_element_type=F32)


def _mm_res(x, a, w, tm, tn):
    m, n = x.shape
    kd = a.shape[1]
    vmem = 2 * tm * kd * a.dtype.itemsize + 2 * kd * tn * 2 + 4 * tm * tn * 4 + tm * kd * 2 + tm * tn * 4
    return pl.pallas_call(
        _mm_res_kernel,
        grid=(m // tm, n // tn),
        in_specs=[pl.BlockSpec((tm, tn), lambda i, j: (i, j)),
                  pl.BlockSpec((tm, kd), lambda i, j: (i, 0)),
                  pl.BlockSpec((kd, tn), lambda i, j: (0, j))],
        out_specs=pl.BlockSpec((tm, tn), lambda i, j: (i, j)),
        out_shape=jax.ShapeDtypeStruct((m, n), F32),
        compiler_params=_params(("parallel", "arbitrary"), vmem),
        name="mm_res",
    )(x, a, w)


def _ffn_kernel(x_ref, g_ref, w1_ref, w2_ref, o_ref, h_scr):
    @pl.when(pl.program_id(1) == 0)
    def _():
        x = x_ref[...]
        h_scr[...] = _rms_rows(x, g_ref[...], NORM_EPS).astype(BF16)
        o_ref[...] = x

    a = jnp.dot(h_scr[...], w1_ref[...], preferred_element_type=F32)
    a = jnp.maximum(a, 0.0)
    a = (a * a).astype(BF16)
    o_ref[...] += jnp.dot(a, w2_ref[...], preferred_element_type=F32)


def _ffn(x, g, w1, w2, layer, tm, tf):
    m, d = x.shape
    dff = w1.shape[2]
    vmem = 4 * tm * d * 4 + tm * d * 2 + 4 * d * tf * 2 + tm * tf * 8
    return pl.pallas_call(
        _ffn_kernel,
        grid=(m // tm, dff // tf),
        in_specs=[pl.BlockSpec((tm, d), lambda i, f: (i, 0)),
                  pl.BlockSpec((1, d), lambda i, f: (0, 0)),
                  pl.BlockSpec((None, d, tf), lambda i, f: (layer, 0, f)),
                  pl.BlockSpec((None, tf, d), lambda i, f: (layer, f, 0))],
        out_specs=pl.BlockSpec((tm, d), lambda i, f: (i, 0)),
        out_shape=jax.ShapeDtypeStruct((m, d), F32),
        scratch_shapes=[pltpu.VMEM((tm, d), BF16)],
        compiler_params=_params(("parallel", "arbitrary"), vmem),
        name="ffn",
    )(x, g[None, :], w1, w2)


def _gelu(x):
    return 0.5 * x * (1.0 + lax.erf(x * (2.0 ** -0.5)))


def _gmlp_kernel(x_ref, g_ref, wu_ref, wv_ref, vg_ref, bsum_ref, ws_ref, bs_ref, *rest, emit_cv):
    if emit_cv:
        gate_ref, cv_ref, h_scr = rest
    else:
        gate_ref, h_scr = rest

    @pl.when(pl.program_id(1) == 0)
    def _():
        h_scr[...] = _rms_rows(x_ref[...], g_ref[...], NORM_EPS).astype(BF16)

    h = h_scr[...]
    tm = h.shape[0]
    tn = wu_ref.shape[1]
    u = _gelu(jnp.dot(h, wu_ref[...], preferred_element_type=F32))
    v = _gelu(jnp.dot(h, wv_ref[...], preferred_element_type=F32))
    vn = _group_rms(v, bsum_ref[...], GROUP_DIM, NORM_EPS) * vg_ref[...]
    if emit_cv:
        cv_ref[...] = vn
    vb = vn.astype(BF16)
    t_idx = lax.broadcasted_iota(jnp.int32, (CHUNK, CHUNK), 0)
    s_idx = lax.broadcasted_iota(jnp.int32, (CHUNK, CHUNK), 1)
    causal = t_idx >= s_idx
    for gi in range(tn // GROUP_DIM):
        cols = slice(gi * GROUP_DIM, (gi + 1) * GROUP_DIM)
        w = jnp.where(causal, ws_ref[gi], 0.0).astype(BF16)
        bias = bs_ref[gi]
        for c in range(tm // CHUNK):
            rows = slice(c * CHUNK, (c + 1) * CHUNK)
            mixed = jnp.dot(w, vb[rows, cols], preferred_element_type=F32) + bias
            gate_ref[rows, cols] = (u[rows, cols] * mixed).astype(BF16)


def _gmlp_front(x, g, w_in, v_gain, ws, bs, tm, tn, emit_cv):
    m, d = x.shape
    dg = w_in.shape[1] // 2
    nj = dg // tn
    gpt = tn // GROUP_DIM
    blk = pl.BlockSpec((tm, tn), lambda i, j: (i, j))
    out_specs = [blk, blk] if emit_cv else [blk]
    out_shape = [jax.ShapeDtypeStruct((m, dg), BF16)]
    if emit_cv:
        out_shape.append(jax.ShapeDtypeStruct((m, dg), F32))
    vmem = 2 * tm * d * 4 + tm * d * 2 + 4 * d * tn * 2 + 4 * gpt * CHUNK * CHUNK * 4 + 12 * tm * tn * 4
    res = pl.pallas_call(
        functools.partial(_gmlp_kernel, emit_cv=emit_cv),
        grid=(m // tm, nj),
        in_specs=[pl.BlockSpec((tm, d), lambda i, j: (i, 0)),
                  pl.BlockSpec((1, d), lambda i, j: (0, 0)),
                  pl.BlockSpec((d, tn), lambda i, j: (0, j)),
                  pl.BlockSpec((d, tn), lambda i, j, nj=nj: (0, j + nj)),
                  pl.BlockSpec((1, tn), lambda i, j: (0, j)),
                  pl.BlockSpec((MXU_DIM, MXU_DIM), lambda i, j: (0, 0)),
                  pl.BlockSpec((gpt, CHUNK, CHUNK), lambda i, j: (j, 0, 0)),
                  pl.BlockSpec((gpt, CHUNK, GROUP_DIM), lambda i, j: (j, 0, 0))],
        out_specs=out_specs,
        out_shape=out_shape,
        scratch_shapes=[pltpu.VMEM((tm, d), BF16)],
        compiler_params=_params(("parallel", "arbitrary"), vmem),
        name="gmlp_front",
    )(x, g[None, :], w_in, w_in, v_gain[None, :], _group_sum_matrix(GROUP_DIM), ws, bs)
    return res if emit_cv else (res[0], None)


def _shifted_bias(dist, table_ref, head):
    max_exact = NUM_BUCKETS // 2
    n = jnp.maximum(dist, 0)
    nf = jnp.maximum(n, 1).astype(F32)
    large = max_exact + (jnp.log(nf / max_exact) / math.log(MAX_DISTANCE / max_exact)
                         * (NUM_BUCKETS - max_exact)).astype(jnp.int32)
    bucket = jnp.where(n < max_exact, n, jnp.minimum(large, NUM_BUCKETS - 1))
    far = table_ref[NUM_BUCKETS - 1, head]
    acc = jnp.zeros(dist.shape, F32)
    for b in range(NUM_BUCKETS - 1):
        acc = jnp.where(bucket == b, table_ref[b, head] - far, acc)
    return acc


def _prompt_bias_kernel(table_ref, o_ref, *, tile):
    head = pl.program_id(0)
    key = lax.broadcasted_iota(jnp.int32, (LANES, LANES), 0)
    qry = lax.broadcasted_iota(jnp.int32, (LANES, LANES), 1)
    d0 = jnp.where(qry >= key, _shifted_bias(qry - key, table_ref, head), NEG_INF)
    d1 = _shifted_bias(LANES + qry - key, table_ref, head)
    nb = tile // LANES
    zeros = jnp.zeros((LANES, LANES), F32)
    masked = jnp.full((LANES, LANES), NEG_INF, F32)
    for a in range(nb):
        for b in range(nb):
            rows = slice(a * LANES, (a + 1) * LANES)
            cols = slice(b * LANES, (b + 1) * LANES)
            diag = d0 if b == a else d1 if b == a + 1 else zeros if b > a else masked
            o_ref[0, 0, rows, cols] = diag
            o_ref[0, 1, rows, cols] = d1 if (a == nb - 1 and b == 0) else zeros


def _prompt_bias(table, n_heads, tile):
    return pl.pallas_call(
        functools.partial(_prompt_bias_kernel, tile=tile),
        grid=(n_heads,),
        in_specs=[pl.BlockSpec(memory_space=pltpu.SMEM)],
        out_specs=pl.BlockSpec((1, 2, tile, tile), lambda h: (h, 0, 0, 0)),
        out_shape=jax.ShapeDtypeStruct((n_heads, 2, tile, tile), F32),
        compiler_params=_params(("parallel",), 8 * tile * tile * 4),
        name="prompt_bias",
    )(table)


def _sample_bias_kernel(table_ref, last_ref, new_ref, *, n_heads, dec_seq, page):
    rows = 2 * n_heads * dec_seq
    ridx = lax.broadcasted_iota(jnp.int32, (rows, page), 0)
    key = lax.broadcasted_iota(jnp.int32, (rows, page), 1)
    tok = ridx % dec_seq
    row_head = (ridx % (n_heads * dec_seq)) // dec_seq
    last = jnp.zeros((rows, page), F32)
    new = jnp.zeros((rows, page), F32)
    for h in range(n_heads):
        last = jnp.where(row_head == h, _shifted_bias(page + tok - key, table_ref, h), last)
        new = jnp.where(row_head == h, _shifted_bias(tok - key, table_ref, h), new)
    last_ref[...] = last
    new_ref[...] = jnp.where(key <= tok, new, NEG_INF)


def _sample_bias(table, n_heads, dec_seq, page):
    rows = 2 * n_heads * dec_seq
    return pl.pallas_call(
        functools.partial(_sample_bias_kernel, n_heads=n_heads, dec_seq=dec_seq, page=page),
        in_specs=[pl.BlockSpec(memory_space=pltpu.SMEM)],
        out_specs=[pl.BlockSpec(memory_space=pltpu.VMEM), pl.BlockSpec(memory_space=pltpu.VMEM)],
        out_shape=[jax.ShapeDtypeStruct((rows, page), F32), jax.ShapeDtypeStruct((rows, page), F32)],
        name="sample_bias",
    )(table)


def _attn_prompt_kernel(qt_ref, k_ref, vt_ref, bias_ref, lam_ref, sg_ref, o_ref,
                        qs_scr, m_scr, l_scr, acc_scr, *, layer):
    i = pl.program_id(2)
    t = qt_ref.shape[2]
    hd = QK_HEAD_DIM
    dv = V_HEAD_DIM
    heads = qs_scr.shape[0]
    zero = jnp.zeros((hd, t), BF16)
    for h in range(heads):
        qs_scr[h, 0:hd, 0:t] = qt_ref[0, h * dv:h * dv + hd, :]
        qs_scr[h, hd:2 * hd, 0:t] = zero
        qs_scr[h, 0:hd, t:2 * t] = zero
        qs_scr[h, hd:2 * hd, t:2 * t] = qt_ref[0, h * dv + hd:(h + 1) * dv, :]
    m_scr[...] = jnp.full(m_scr.shape, NEG_INF, F32)
    l_scr[...] = jnp.zeros(l_scr.shape, F32)
    acc_scr[...] = jnp.zeros(acc_scr.shape, F32)

    far, prev, diag = 0, 1, 2
    qc = MXU_DIM
    per_map = t // qc
    chunks = [(h, c) for c in range(2 * per_map) for h in range(heads)]

    def block(j, kind):
        base = pl.multiple_of(j * t, t)

        def n_keys(c):
            return (c % per_map + 1) * qc if kind == diag else t

        def scores(h, c):
            nk, cq = n_keys(c), c % per_map
            k = k_ref[pl.ds(base, nk), h * dv:(h + 1) * dv]
            s = jnp.dot(k, qs_scr[h, :, c * qc:(c + 1) * qc], preferred_element_type=F32)
            if kind == diag:
                s = s + bias_ref[h, 0, 0:nk, cq * qc:(cq + 1) * qc]
            elif kind == prev and cq == 0:
                s = s + bias_ref[h, 1, :, 0:qc]
            return s

        def softmax_update(h, c, s):
            cols = slice(c * qc, (c + 1) * qc)
            m_prev = m_scr[h, :, cols]
            m_new = jnp.maximum(m_prev, jnp.max(s, axis=0, keepdims=True))
            alpha = jnp.exp(m_prev - m_new)
            p = jnp.exp(s - m_new)
            l_scr[h, :, cols] = alpha * l_scr[h, :, cols] + jnp.sum(p, axis=0, keepdims=True)
            m_scr[h, :, cols] = m_new
            return p.astype(BF16), alpha

        def weigh(h, c, p, alpha):
            cols = slice(c * qc, (c + 1) * qc)
            acc_scr[h, :, cols] = alpha * acc_scr[h, :, cols] + jnp.dot(
                vt_ref[j, h * dv:(h + 1) * dv, 0:n_keys(c)], p, preferred_element_type=F32)

        ahead = ATTN_SCORES_AHEAD
        s = {n: scores(*chunks[n]) for n in range(min(ahead, len(chunks)))}
        for n, (h, c) in enumerate(chunks):
            p, alpha = softmax_update(h, c, s.pop(n))
            if n + ahead < len(chunks):
                s[n + ahead] = scores(*chunks[n + ahead])
            weigh(h, c, p, alpha)

    def far_block(j, carry):
        block(j, far)
        return carry

    lax.fori_loop(0, i - 1, far_block, 0)

    @pl.when(i >= 1)
    def _():
        block(i - 1, prev)

    block(i, diag)
    lam = _diff_lambda_value(lam_ref, layer)
    for h in range(heads):
        o = acc_scr[h] / l_scr[h]
        od = o[:, 0:t] - lam * o[:, t:2 * t]
        on = od * lax.rsqrt(jnp.mean(od * od, axis=0, keepdims=True) + SUBLN_EPS)
        o_ref[:, h * dv:(h + 1) * dv] = (on.T * sg_ref[...] * (1.0 - _lambda_init(layer))).astype(BF16)


def _attn_prompt(qt, kb, vt, bias, lam, subln, batch, layer):
    n_tiles, d, tile = qt.shape
    m = kb.shape[0]
    seq = m // batch
    nq = seq // tile
    heads = ATTN_HEADS_PER_STEP
    width = heads * V_HEAD_DIM
    assert d % width == 0
    vmem = (4 * seq * width * 2 + 4 * heads * tile * tile * 4 + 8 * tile * MXU_DIM * 4
            + 4 * width * 2 * tile * 4)
    return pl.pallas_call(
        functools.partial(_attn_prompt_kernel, layer=layer),
        grid=(batch, d // width, nq),
        in_specs=[pl.BlockSpec((1, width, tile), lambda b, g, i: (b * nq + i, g, 0)),
                  pl.BlockSpec((seq, width), lambda b, g, i: (b, g)),
                  pl.BlockSpec((nq, width, tile), lambda b, g, i: (b, g, 0)),
                  pl.BlockSpec((heads, 2, tile, tile), lambda b, g, i: (g, 0, 0, 0)),
                  pl.BlockSpec(lam.shape, lambda b, g, i: (0, 0)),
                  pl.BlockSpec((1, V_HEAD_DIM), lambda b, g, i: (0, 0))],
        out_specs=pl.BlockSpec((tile, width), lambda b, g, i: (b * nq + i, g)),
        out_shape=jax.ShapeDtypeStruct((m, d), BF16),
        scratch_shapes=[pltpu.VMEM((heads, V_HEAD_DIM, 2 * tile), BF16),
                        pltpu.VMEM((heads, 1, 2 * tile), F32),
                        pltpu.VMEM((heads, 1, 2 * tile), F32),
                        pltpu.VMEM((heads, V_HEAD_DIM, 2 * tile), F32)],
        compiler_params=_params(("parallel", "parallel", "arbitrary"), vmem),
        name="attn_prompt",
    )(qt, kb, vt, bias, lam, subln[None, :])


def _attn_sample_kernel(pt_ref, q_ref, kn_ref, vn_ref, *rest, layer, n_pages, dec_seq, slots):
    d = q_ref.shape[2]
    n_heads = d // V_HEAD_DIM
    v_parts = n_heads // VALUE_TILE_HEADS
    kc_refs, vc_refs = rest[:slots], rest[slots:slots + slots * v_parts]
    (blast_ref, bnew_ref, lam_ref, sg_ref, o_ref,
     qbd_scr, lg_scr, w_scr, wn_scr, acc_scr) = rest[slots + slots * v_parts:]
    s = pl.program_id(1)
    k_steps = n_pages // slots
    rows = 2 * n_heads * dec_seq
    half = rows // 2
    page = kc_refs[0].shape[1]
    group_rows = acc_scr.shape[1]
    group_heads = group_rows // dec_seq

    def padded_new(ref):
        x = ref[0]
        return jnp.concatenate([x, jnp.zeros((page - x.shape[0], d), F32)], axis=0).astype(BF16)

    @pl.when(s == 0)
    def _():
        q = q_ref[0]
        qrep = jnp.broadcast_to(q[None], (rows // q.shape[0], q.shape[0], d)).reshape(rows, d)
        ridx = lax.broadcasted_iota(jnp.int32, (rows, d), 0)
        lane = lax.broadcasted_iota(jnp.int32, (rows, d), 1)
        row_head = (ridx % half) // dec_seq
        row_map = ridx // half
        match = (lane // V_HEAD_DIM == row_head) & ((lane // QK_HEAD_DIM) % 2 == row_map)
        qbd_scr[...] = jnp.where(match, qrep, 0.0).astype(BF16)
        acc_scr[...] = jnp.zeros(acc_scr.shape, F32)

    @pl.when(s < k_steps)
    def _():
        for e in range(slots):
            lg_scr[s * slots + e] = jnp.dot(qbd_scr[...], kc_refs[e][...].astype(BF16),
                                            preferred_element_type=F32)

    @pl.when(s == k_steps - 1)
    def _():
        lam = _diff_lambda_value(lam_ref, layer)
        lg_scr[n_pages - 1] = lg_scr[n_pages - 1] + blast_ref[...]
        lgn = lax.dot_general(qbd_scr[...], padded_new(kn_ref), (((1,), (1,)), ((), ())),
                              preferred_element_type=F32) + bnew_ref[...]
        lg = lg_scr[...]
        m = jnp.maximum(jnp.max(jnp.max(lg, axis=0), axis=-1, keepdims=True),
                        jnp.max(lgn, axis=-1, keepdims=True))
        p = jnp.exp(lg - m[None])
        pn = jnp.exp(lgn - m)
        l = jnp.sum(jnp.sum(p, axis=0), axis=-1, keepdims=True) + jnp.sum(pn, axis=-1, keepdims=True)
        inv = 1.0 / l
        p = p * inv[None]
        pn = pn * inv
        w_scr[...] = p[:, 0:half, :] - lam * p[:, half:rows, :]
        wn_scr[...] = (pn[0:half, :] - lam * pn[half:rows, :]).astype(BF16)

    @pl.when(s >= k_steps)
    def _():
        for g in range(n_heads // group_heads):
            part, first = divmod(g * group_heads, VALUE_TILE_HEADS)
            acc = acc_scr[g]
            for e in range(slots):
                vflat = vc_refs[e * v_parts + part].reshape(page * VALUE_TILE_HEADS, V_HEAD_DIM)
                v = jnp.concatenate(
                    [vflat[pl.ds(first + hh, page, stride=VALUE_TILE_HEADS), :].astype(BF16)
                     for hh in range(group_heads)], axis=1)
                w = w_scr[(s - k_steps) * slots + e, g * group_rows:(g + 1) * group_rows, :]
                acc = acc + jnp.dot(w.astype(BF16), v, preferred_element_type=F32)
            acc_scr[g] = acc

    @pl.when(s == 2 * k_steps - 1)
    def _():
        new = jnp.dot(wn_scr[...], padded_new(vn_ref), preferred_element_type=F32)
        scale = 1.0 - _lambda_init(layer)
        for h in range(n_heads):
            g, e = divmod(h, group_heads)
            cols = slice(h * V_HEAD_DIM, (h + 1) * V_HEAD_DIM)
            piece = (acc_scr[g, e * dec_seq:(e + 1) * dec_seq, e * V_HEAD_DIM:(e + 1) * V_HEAD_DIM]
                     + new[h * dec_seq:(h + 1) * dec_seq, cols])
            o_ref[0, :, cols] = _rms_rows(piece, sg_ref[...], SUBLN_EPS) * scale


def _attn_sample(q, k_new, v_new, cache_k, cache_v, page_table, blast, bnew, lam, subln, layer):
    batch, dec_seq, d = q.shape
    n_pages = page_table.shape[1]
    page = cache_k.shape[2]
    n_heads = d // V_HEAD_DIM
    rows = 2 * n_heads * dec_seq
    sub_f32, sub_bf16 = 8, 16
    assert sub_f32 % dec_seq == 0
    q = jnp.tile(q, (1, sub_f32 // dec_seq, 1))
    k_new = jnp.pad(k_new, ((0, 0), (0, sub_bf16 - dec_seq), (0, 0)))
    v_new = jnp.pad(v_new, ((0, 0), (0, sub_bf16 - dec_seq), (0, 0)))
    tok = lambda n: pl.BlockSpec((1, n, d), lambda b, s, pt: (b, 0, 0))
    full = lambda a: pl.BlockSpec(a.shape, lambda b, s, pt: (0,) * a.ndim)
    slots = SAMPLE_PAGE_SLOTS
    assert n_pages % slots == 0
    k_steps = n_pages // slots

    def key_page(e):
        return lambda b, s, pt: (pt[b, jnp.minimum(s, k_steps - 1) * slots + e], 0, 0)

    def value_page(e, part):
        def index(b, s, pt):
            weighting = s >= k_steps
            row = jnp.where(weighting, b, jnp.maximum(b - 1, 0))
            col = jnp.where(weighting, s - k_steps, k_steps - 1) * slots + e
            return (pt[row, col], 0, part, 0)
        return index

    v_parts = n_heads // VALUE_TILE_HEADS
    grid_spec = pltpu.PrefetchScalarGridSpec(
        num_scalar_prefetch=1,
        grid=(batch, 2 * k_steps),
        in_specs=[tok(sub_f32), tok(sub_bf16), tok(sub_bf16)]
                 + [pl.BlockSpec((None, d, page), key_page(e)) for e in range(slots)]
                 + [pl.BlockSpec((None, page, VALUE_TILE_HEADS, V_HEAD_DIM), value_page(e, part))
                    for e in range(slots) for part in range(v_parts)]
                 + [full(blast), full(bnew), full(lam),
                    pl.BlockSpec((1, V_HEAD_DIM), lambda b, s, pt: (0, 0))],
        out_specs=pl.BlockSpec((1, dec_seq, d), lambda b, s, pt: (b, 0, 0)),
        scratch_shapes=[pltpu.VMEM((rows, d), BF16),
                        pltpu.VMEM((n_pages, rows, page), F32),
                        pltpu.VMEM((n_pages, rows // 2, page), F32),
                        pltpu.VMEM((rows // 2, page), BF16),
                        pltpu.VMEM((n_heads * dec_seq // sub_f32, sub_f32,
                                    sub_f32 // dec_seq * V_HEAD_DIM), F32)],
    )
    vmem = 4 * slots * page * d * 4 + 6 * n_pages * rows * page * 4 + 4 * page * d * 2
    return pl.pallas_call(
        functools.partial(_attn_sample_kernel, layer=layer, n_pages=n_pages, dec_seq=dec_seq, slots=slots),
        grid_spec=grid_spec,
        out_shape=jax.ShapeDtypeStruct((batch, dec_seq, d), F32),
        compiler_params=_params(("arbitrary", "arbitrary"), vmem),
        name="attn_sample",
    )(page_table, q, k_new, v_new, *([cache_k] * slots), *([cache_v] * (slots * v_parts)),
      blast, bnew, lam, subln[None, :])


PROMPT_TM = 512
ATTN_TILE = 512
PROJ_TN = 512
RES_TN = 1024
FFN_TF = 1024
SAMPLE_PAGE_SLOTS = 8
ATTN_HEADS_PER_STEP = 2
ATTN_SCORES_AHEAD = 4


def kernel(x_prompt, x_sample, cache_k, cache_v, page_table, rel_bias_table, mix_norm, ffn_norm,
           attn_w_qkv, attn_w_o, attn_q_norm, attn_k_norm, attn_lambda, attn_subln,
           cm_w_in, cm_v_norm, cm_w_s, cm_b_s, cm_w_o, ffn_w1, ffn_w2):
    batch, seq, d = x_prompt.shape
    dec_batch, dec_seq, _ = x_sample.shape
    depth = mix_norm.shape[0]
    n_heads = d // V_HEAD_DIM
    n_groups = cm_w_s.shape[1]
    page = cache_k.shape[2]
    mp, ms = batch * seq, dec_batch * dec_seq
    assert seq % ATTN_TILE == 0 and mp % PROMPT_TM == 0 and ms % CHUNK == 0 and CHUNK % dec_seq == 0

    xp = x_prompt.reshape(mp, d)
    xs = x_sample.reshape(ms, d)
    ck = jnp.transpose(cache_k, (0, 1, 3, 4, 5, 2)).reshape(cache_k.shape[0], cache_k.shape[1], d, page)
    cv = cache_v

    bias_p = _prompt_bias(rel_bias_table, n_heads, ATTN_TILE)
    blast, bnew = _sample_bias(rel_bias_table, n_heads, dec_seq, page)

    w1_all = ffn_w1.astype(BF16)
    w2_all = ffn_w2.astype(BF16)
    k_prompt, v_prompt, k_sample, v_sample, cv_sample = [], [], [], [], []
    for i in range(depth):
        j = i // N_MIXERS
        if i % N_MIXERS == 0:
            w_qkv = attn_w_qkv[j].astype(BF16)
            w_o = attn_w_o[j].astype(BF16)
            qtp, kbp, ktp, vp, vtp = _qkv(xp, mix_norm[i], w_qkv, attn_q_norm[j], attn_k_norm[j],
                                          ATTN_TILE, PROJ_TN, seq=seq)
            qs, ks, vs = _qkv(xs, mix_norm[i], w_qkv, attn_q_norm[j], attn_k_norm[j], ms, PROJ_TN)
            op = _attn_prompt(qtp, kbp, vtp, bias_p, attn_lambda[j], attn_subln[j], batch, i)
            os_ = _attn_sample(qs.reshape(dec_batch, dec_seq, d),
                               ks.reshape(dec_batch, dec_seq, d), vs.reshape(dec_batch, dec_seq, d),
                               ck[j], cv[j], page_table, blast, bnew, attn_lambda[j], attn_subln[j], i)
            xp = _mm_res(xp, op, w_o, 2 * PROMPT_TM, RES_TN)
            xs = _mm_res(xs, os_.reshape(ms, d), w_o, ms, RES_TN)
            k_prompt.append(jnp.transpose(ktp.reshape(batch, n_heads, 2, QK_HEAD_DIM, seq), (0, 4, 1, 2, 3)))
            v_prompt.append(vp.reshape(batch, seq, n_heads, V_HEAD_DIM))
            k_sample.append(ks.reshape(dec_batch, dec_seq, n_heads, 2, QK_HEAD_DIM))
            v_sample.append(vs.reshape(dec_batch, dec_seq, n_heads, V_HEAD_DIM))
        else:
            w_in = cm_w_in[j].astype(BF16)
            w_o = cm_w_o[j].astype(BF16)
            bs_p = jnp.broadcast_to(cm_b_s[j][:, :, None], (n_groups, CHUNK, GROUP_DIM))
            gp, _ = _gmlp_front(xp, mix_norm[i], w_in, cm_v_norm[j], cm_w_s[j], bs_p,
                                PROMPT_TM, PROJ_TN, False)
            reps = CHUNK // dec_seq
            eye = jnp.eye(reps, dtype=F32)
            ws_s = (eye[None, :, None, :, None] * cm_w_s[j][:, None, :dec_seq, None, :dec_seq]
                    ).reshape(n_groups, CHUNK, CHUNK)
            bs_s = jnp.broadcast_to(jnp.tile(cm_b_s[j][:, :dec_seq], (1, reps))[:, :, None],
                                    (n_groups, CHUNK, GROUP_DIM))
            gs, cvs = _gmlp_front(xs, mix_norm[i], w_in, cm_v_norm[j], ws_s, bs_s, ms, PROJ_TN, True)
            xp = _mm_res(xp, gp, w_o, 2 * PROMPT_TM, RES_TN)
            xs = _mm_res(xs, gs, w_o, ms, RES_TN)
            cv_sample.append(cvs.reshape(dec_batch, dec_seq, n_groups, GROUP_DIM))
        xp = _ffn(xp, ffn_norm[i], w1_all, w2_all, i, PROMPT_TM, FFN_TF)
        xs = _ffn(xs, ffn_norm[i], w1_all, w2_all, i, ms, FFN_TF)

    return (xp.reshape(batch, seq, d), xs.reshape(dec_batch, dec_seq, d),
            jnp.stack(k_prompt), jnp.stack(v_prompt), jnp.stack(k_sample), jnp.stack(v_sample),
            jnp.stack(cv_sample))
```

```python
import functools
import math

import jax
import jax.numpy as jnp
from jax import lax
from jax.experimental import pallas as pl
from jax.experimental.pallas import tpu as pltpu

F32 = jnp.float32
BF16 = jnp.bfloat16

QK_HEAD_DIM = 64
V_HEAD_DIM = 128
NUM_BUCKETS = 32
MAX_DISTANCE = 128
CHUNK = 128
GROUP_DIM = 128
N_MIXERS = 2
NORM_EPS = 1e-6
SUBLN_EPS = 1e-5
NEG_INF = -1e30
LOG2_E = 1.4426950408889634

LANES = 128
VALUE_TILE_HEADS = 8
MXU_DIM = 256
VMEM_CAP = 56 * 1024 * 1024


def _lambda_init(layer):
    return 0.8 - 0.6 * math.exp(-0.3 * layer)


def _params(semantics, vmem_bytes):
    limit = int(min(VMEM_CAP, max(32 * 1024 * 1024, vmem_bytes * 5 // 4)))
    return pltpu.CompilerParams(dimension_semantics=semantics, vmem_limit_bytes=limit)


def _rms_rows(x, g, eps):
    ms = jnp.mean(x * x, axis=-1, keepdims=True)
    return x * lax.rsqrt(ms + eps) * g


def _group_rms(x, bsum, group, eps):
    x2 = x * x
    hi = x2.astype(BF16)
    lo = (x2 - hi.astype(F32)).astype(BF16)
    parts = []
    for c in range(x.shape[1] // MXU_DIM):
        sl = slice(c * MXU_DIM, (c + 1) * MXU_DIM)
        parts.append(jnp.dot(hi[:, sl], bsum, preferred_element_type=F32)
                     + jnp.dot(lo[:, sl], bsum, preferred_element_type=F32))
    ss = parts[0] if len(parts) == 1 else jnp.concatenate(parts, axis=1)
    return x * lax.rsqrt(ss * (1.0 / group) + eps)


def _group_sum_matrix(group):
    r = jnp.arange(MXU_DIM) // group
    return (r[:, None] == r[None, :]).astype(BF16)


def _diff_lambda_value(lam_ref, layer):
    lf = lam_ref[...]
    a = jnp.sum(lf[0:1] * lf[1:2], axis=-1, keepdims=True)
    b = jnp.sum(lf[2:3] * lf[3:4], axis=-1, keepdims=True)
    return jnp.exp(a) - jnp.exp(b) + _lambda_init(layer)


def _qkv_kernel(x_ref, g_ref, wq_ref, wk_ref, wv_ref, qg_ref, kg_ref, bsum_ref, *rest, feature_major):
    if feature_major:
        qt_ref, kb_ref, kt_ref, v_ref, vt_ref, h_scr = rest
    else:
        q_ref, k_ref, v_ref, h_scr = rest

    @pl.when(pl.program_id(1) == 0)
    def _():
        h_scr[...] = _rms_rows(x_ref[...], g_ref[...], NORM_EPS).astype(BF16)

    h = h_scr[...]
    bsum = bsum_ref[...]
    q_scale = QK_HEAD_DIM ** -0.5 * (LOG2_E if feature_major else 1.0)
    q = jnp.dot(h, wq_ref[...], preferred_element_type=F32)
    qn = _group_rms(q, bsum, QK_HEAD_DIM, NORM_EPS) * qg_ref[...] * q_scale
    k = jnp.dot(h, wk_ref[...], preferred_element_type=F32)
    kn = _group_rms(k, bsum, QK_HEAD_DIM, NORM_EPS) * kg_ref[...]
    v = jnp.dot(h, wv_ref[...], preferred_element_type=F32)
    v_ref[...] = v
    if feature_major:
        qt_ref[0] = qn.T.astype(BF16)
        kb_ref[...] = kn.astype(BF16)
        kt_ref[0] = kn.T
        vt_ref[0] = v.T.astype(BF16)
    else:
        q_ref[...] = qn
        k_ref[...] = kn


def _qkv(x, g, w_qkv, q_gain, k_gain, tm, tn, seq=None):
    m, d = x.shape
    n = w_qkv.shape[1] // 3
    nj = n // tn
    qg = jnp.tile(q_gain, tn // QK_HEAD_DIM)[None, :]
    kg = jnp.tile(k_gain, tn // QK_HEAD_DIM)[None, :]
    bsum = _group_sum_matrix(QK_HEAD_DIM)
    row = pl.BlockSpec((tm, d), lambda i, j: (i, 0))
    col = lambda off: pl.BlockSpec((d, tn), lambda i, j, off=off: (0, j + off))
    vec = lambda width: pl.BlockSpec((1, width), lambda i, j: (0, 0))
    out = pl.BlockSpec((tm, tn), lambda i, j: (i, j))
    if seq is None:
        out_specs = [out, out, out]
        out_shape = [jax.ShapeDtypeStruct((m, n), F32)] * 3
    else:
        per_seq = seq // tm
        tile_t = pl.BlockSpec((1, tn, tm), lambda i, j: (i, j, 0))
        seq_t = pl.BlockSpec((1, tn, tm), lambda i, j: (i // per_seq, j, i % per_seq))
        out_specs = [tile_t, out, seq_t, out, tile_t]
        out_shape = [jax.ShapeDtypeStruct((m // tm, n, tm), BF16), jax.ShapeDtypeStruct((m, n), BF16),
                     jax.ShapeDtypeStruct((m // seq, n, seq), F32), jax.ShapeDtypeStruct((m, n), F32),
                     jax.ShapeDtypeStruct((m // tm, n, tm), BF16)]
    vmem = 2 * tm * d * 4 + tm * d * 2 + 6 * d * tn * 2 + 2 * tm * tn * 16 + 10 * tm * tn * 4
    return pl.pallas_call(
        functools.partial(_qkv_kernel, feature_major=seq is not None),
        grid=(m // tm, nj),
        in_specs=[row, vec(d), col(0), col(nj), col(2 * nj), vec(tn), vec(tn),
                  pl.BlockSpec((MXU_DIM, MXU_DIM), lambda i, j: (0, 0))],
        out_specs=out_specs,
        out_shape=out_shape,
        scratch_shapes=[pltpu.VMEM((tm, d), BF16)],
        compiler_params=_params(("parallel", "arbitrary"), vmem),
        name="qkv",
    )(x, g[None, :], w_qkv, w_qkv, w_qkv, qg, kg, bsum)


def _mm_res_kernel(x_ref, a_ref, w_ref, o_ref):
    o_ref[...] = x_ref[...] + jnp.dot(a_ref[...].astype(BF16), w_ref[...],
                                      preferred_element_type=F32)


def _mm_res(x, a, w, tm, tn):
    m, n = x.shape
    kd = a.shape[1]
    vmem = 2 * tm * kd * a.dtype.itemsize + 2 * kd * tn * 2 + 4 * tm * tn * 4 + tm * kd * 2 + tm * tn * 4
    return pl.pallas_call(
        _mm_res_kernel,
        grid=(m // tm, n // tn),
        in_specs=[pl.BlockSpec((tm, tn), lambda i, j: (i, j)),
                  pl.BlockSpec((tm, kd), lambda i, j: (i, 0)),
                  pl.BlockSpec((kd, tn), lambda i, j: (0, j))],
        out_specs=pl.BlockSpec((tm, tn), lambda i, j: (i, j)),
        out_shape=jax.ShapeDtypeStruct((m, n), F32),
        compiler_params=_params(("parallel", "arbitrary"), vmem),
        name="mm_res",
    )(x, a, w)


def _ffn_kernel(x_ref, g_ref, w1_ref, w2_ref, o_ref, *rest, emit_weights):
    if emit_weights:
        w1b_ref, w2b_ref, h_scr = rest
    else:
        (h_scr,) = rest

    @pl.when(pl.program_id(1) == 0)
    def _():
        x = x_ref[...]
        h_scr[...] = _rms_rows(x, g_ref[...], NORM_EPS).astype(BF16)
        o_ref[...] = x

    w1 = w1_ref[...].astype(BF16)
    w2 = w2_ref[...].astype(BF16)
    if emit_weights:
        w1b_ref[...] = w1
        w2b_ref[...] = w2
    a = jnp.dot(h_scr[...], w1, preferred_element_type=F32)
    a = jnp.maximum(a, 0.0)
    a = (a * a).astype(BF16)
    o_ref[...] += jnp.dot(a, w2, preferred_element_type=F32)


def _ffn(x, g, w1, w2, tm, tf, layer=None):
    m, d = x.shape
    dff = w1.shape[-1]
    emit = layer is not None
    assert not emit or m == tm
    if emit:
        w1_spec = pl.BlockSpec((None, d, tf), lambda i, f: (layer, 0, f))
        w2_spec = pl.BlockSpec((None, tf, d), lambda i, f: (layer, f, 0))
    else:
        w1_spec = pl.BlockSpec((d, tf), lambda i, f: (0, f))
        w2_spec = pl.BlockSpec((tf, d), lambda i, f: (f, 0))
    out_specs = [pl.BlockSpec((tm, d), lambda i, f: (i, 0))]
    out_shape = [jax.ShapeDtypeStruct((m, d), F32)]
    if emit:
        out_specs += [pl.BlockSpec((d, tf), lambda i, f: (0, f)), pl.BlockSpec((tf, d), lambda i, f: (f, 0))]
        out_shape += [jax.ShapeDtypeStruct((d, dff), BF16), jax.ShapeDtypeStruct((dff, d), BF16)]
    vmem = (4 * tm * d * 4 + tm * d * 2 + 4 * d * tf * w1.dtype.itemsize + tm * tf * 8
            + (6 * d * tf * 2 if emit else 0))
    res = pl.pallas_call(
        functools.partial(_ffn_kernel, emit_weights=emit),
        grid=(m // tm, dff // tf),
        in_specs=[pl.BlockSpec((tm, d), lambda i, f: (i, 0)),
                  pl.BlockSpec((1, d), lambda i, f: (0, 0)),
                  w1_spec, w2_spec],
        out_specs=out_specs,
        out_shape=out_shape,
        scratch_shapes=[pltpu.VMEM((tm, d), BF16)],
        compiler_params=_params(("parallel", "arbitrary"), vmem),
        name="ffn",
    )(x, g[None, :], w1, w2)
    return res if emit else res[0]


def _gelu(x):
    return 0.5 * x * (1.0 + lax.erf(x * (2.0 ** -0.5)))


def _gmlp_kernel(x_ref, g_ref, wu_ref, wv_ref, vg_ref, bsum_ref, ws_ref, bs_ref, *rest, emit_cv):
    if emit_cv:
        gate_ref, cv_ref, h_scr = rest
    else:
        gate_ref, h_scr = rest

    @pl.when(pl.program_id(1) == 0)
    def _():
        h_scr[...] = _rms_rows(x_ref[...], g_ref[...], NORM_EPS).astype(BF16)

    h = h_scr[...]
    tm = h.shape[0]
    tn = wu_ref.shape[1]
    u = _gelu(jnp.dot(h, wu_ref[...], preferred_element_type=F32))
    v = _gelu(jnp.dot(h, wv_ref[...], preferred_element_type=F32))
    vn = _group_rms(v, bsum_ref[...], GROUP_DIM, NORM_EPS) * vg_ref[...]
    if emit_cv:
        cv_ref[...] = vn
    vb = vn.astype(BF16)
    t_idx = lax.broadcasted_iota(jnp.int32, (CHUNK, CHUNK), 0)
    s_idx = lax.broadcasted_iota(jnp.int32, (CHUNK, CHUNK), 1)
    causal = t_idx >= s_idx
    for gi in range(tn // GROUP_DIM):
        cols = slice(gi * GROUP_DIM, (gi + 1) * GROUP_DIM)
        w = jnp.where(causal, ws_ref[gi], 0.0).astype(BF16)
        bias = bs_ref[gi]
        for c in range(tm // CHUNK):
            rows = slice(c * CHUNK, (c + 1) * CHUNK)
            mixed = jnp.dot(w, vb[rows, cols], preferred_element_type=F32) + bias
            gate_ref[rows, cols] = (u[rows, cols] * mixed).astype(BF16)


def _gmlp_front(x, g, w_in, v_gain, ws, bs, tm, tn, emit_cv):
    m, d = x.shape
    dg = w_in.shape[1] // 2
    nj = dg // tn
    gpt = tn // GROUP_DIM
    blk = pl.BlockSpec((tm, tn), lambda i, j: (i, j))
    out_specs = [blk, blk] if emit_cv else [blk]
    out_shape = [jax.ShapeDtypeStruct((m, dg), BF16)]
    if emit_cv:
        out_shape.append(jax.ShapeDtypeStruct((m, dg), F32))
    vmem = 2 * tm * d * 4 + tm * d * 2 + 4 * d * tn * 2 + 4 * gpt * CHUNK * CHUNK * 4 + 12 * tm * tn * 4
    res = pl.pallas_call(
        functools.partial(_gmlp_kernel, emit_cv=emit_cv),
        grid=(m // tm, nj),
        in_specs=[pl.BlockSpec((tm, d), lambda i, j: (i, 0)),
                  pl.BlockSpec((1, d), lambda i, j: (0, 0)),
                  pl.BlockSpec((d, tn), lambda i, j: (0, j)),
                  pl.BlockSpec((d, tn), lambda i, j, nj=nj: (0, j + nj)),
                  pl.BlockSpec((1, tn), lambda i, j: (0, j)),
                  pl.BlockSpec((MXU_DIM, MXU_DIM), lambda i, j: (0, 0)),
                  pl.BlockSpec((gpt, CHUNK, CHUNK), lambda i, j: (j, 0, 0)),
                  pl.BlockSpec((gpt, CHUNK, GROUP_DIM), lambda i, j: (j, 0, 0))],
        out_specs=out_specs,
        out_shape=out_shape,
        scratch_shapes=[pltpu.VMEM((tm, d), BF16)],
        compiler_params=_params(("parallel", "arbitrary"), vmem),
        name="gmlp_front",
    )(x, g[None, :], w_in, w_in, v_gain[None, :], _group_sum_matrix(GROUP_DIM), ws, bs)
    return res if emit_cv else (res[0], None)


def _shifted_bias(dist, table_ref, head):
    max_exact = NUM_BUCKETS // 2
    n = jnp.maximum(dist, 0)
    nf = jnp.maximum(n, 1).astype(F32)
    large = max_exact + (jnp.log(nf / max_exact) / math.log(MAX_DISTANCE / max_exact)
                         * (NUM_BUCKETS - max_exact)).astype(jnp.int32)
    bucket = jnp.where(n < max_exact, n, jnp.minimum(large, NUM_BUCKETS - 1))
    far = table_ref[NUM_BUCKETS - 1, head]
    acc = jnp.zeros(dist.shape, F32)
    for b in range(NUM_BUCKETS - 1):
        acc = jnp.where(bucket == b, table_ref[b, head] - far, acc)
    return acc


def _prompt_bias_kernel(table_ref, o_ref, *, tile):
    head = pl.program_id(0)
    key = lax.broadcasted_iota(jnp.int32, (LANES, LANES), 0)
    qry = lax.broadcasted_iota(jnp.int32, (LANES, LANES), 1)
    d0 = jnp.where(qry >= key, _shifted_bias(qry - key, table_ref, head) * LOG2_E, NEG_INF)
    d1 = _shifted_bias(LANES + qry - key, table_ref, head) * LOG2_E
    nb = tile // LANES
    zeros = jnp.zeros((LANES, LANES), F32)
    masked = jnp.full((LANES, LANES), NEG_INF, F32)
    for a in range(nb):
        for b in range(nb):
            rows = slice(a * LANES, (a + 1) * LANES)
            cols = slice(b * LANES, (b + 1) * LANES)
            diag = d0 if b == a else d1 if b == a + 1 else zeros if b > a else masked
            o_ref[0, 0, rows, cols] = diag
            o_ref[0, 1, rows, cols] = d1 if (a == nb - 1 and b == 0) else zeros


def _prompt_bias(table, n_heads, tile):
    return pl.pallas_call(
        functools.partial(_prompt_bias_kernel, tile=tile),
        grid=(n_heads,),
        in_specs=[pl.BlockSpec(memory_space=pltpu.SMEM)],
        out_specs=pl.BlockSpec((1, 2, tile, tile), lambda h: (h, 0, 0, 0)),
        out_shape=jax.ShapeDtypeStruct((n_heads, 2, tile, tile), F32),
        compiler_params=_params(("parallel",), 8 * tile * tile * 4),
        name="prompt_bias",
    )(table)


def _sample_bias_kernel(table_ref, last_ref, new_ref, *, n_heads, dec_seq, page):
    rows = 2 * n_heads * dec_seq
    ridx = lax.broadcasted_iota(jnp.int32, (rows, page), 0)
    key = lax.broadcasted_iota(jnp.int32, (rows, page), 1)
    tok = ridx % dec_seq
    row_head = (ridx % (n_heads * dec_seq)) // dec_seq
    last = jnp.zeros((rows, page), F32)
    new = jnp.zeros((rows, page), F32)
    for h in range(n_heads):
        last = jnp.where(row_head == h, _shifted_bias(page + tok - key, table_ref, h), last)
        new = jnp.where(row_head == h, _shifted_bias(tok - key, table_ref, h), new)
    last_ref[...] = last
    new_ref[...] = jnp.where(key <= tok, new, NEG_INF)


def _sample_bias(table, n_heads, dec_seq, page):
    rows = 2 * n_heads * dec_seq
    return pl.pallas_call(
        functools.partial(_sample_bias_kernel, n_heads=n_heads, dec_seq=dec_seq, page=page),
        in_specs=[pl.BlockSpec(memory_space=pltpu.SMEM)],
        out_specs=[pl.BlockSpec(memory_space=pltpu.VMEM), pl.BlockSpec(memory_space=pltpu.VMEM)],
        out_shape=[jax.ShapeDtypeStruct((rows, page), F32), jax.ShapeDtypeStruct((rows, page), F32)],
        name="sample_bias",
    )(table)


def _attn_prompt_kernel(qt_ref, k_ref, vt_ref, bias_ref, lam_ref, sg_ref, o_ref,
                        qs_scr, m_scr, l_scr, acc_scr, *, layer):
    i = pl.program_id(2)
    t = qt_ref.shape[2]
    hd = QK_HEAD_DIM
    dv = V_HEAD_DIM
    heads = qs_scr.shape[0]
    zero = jnp.zeros((hd, t), BF16)
    for h in range(heads):
        qs_scr[h, 0:hd, 0:t] = qt_ref[0, h * dv:h * dv + hd, :]
        qs_scr[h, hd:2 * hd, 0:t] = zero
        qs_scr[h, 0:hd, t:2 * t] = zero
        qs_scr[h, hd:2 * hd, t:2 * t] = qt_ref[0, h * dv + hd:(h + 1) * dv, :]
    m_scr[...] = jnp.full(m_scr.shape, NEG_INF, F32)
    l_scr[...] = jnp.zeros(l_scr.shape, F32)
    acc_scr[...] = jnp.zeros(acc_scr.shape, F32)

    far, prev, diag = 0, 1, 2
    qc = MXU_DIM
    per_map = t // qc
    chunks = [(h, c) for c in range(2 * per_map) for h in range(heads)]

    def block(j, kind):
        base = pl.multiple_of(j * t, t)

        def n_keys(c):
            return (c % per_map + 1) * qc if kind == diag else t

        def scores(h, c):
            nk, cq = n_keys(c), c % per_map
            k = k_ref[pl.ds(base, nk), h * dv:(h + 1) * dv]
            s = jnp.dot(k, qs_scr[h, :, c * qc:(c + 1) * qc], preferred_element_type=F32)
            if kind == diag:
                s = s + bias_ref[h, 0, 0:nk, cq * qc:(cq + 1) * qc]
            elif kind == prev and cq == 0:
                s = s + bias_ref[h, 1, :, 0:qc]
            return s

        def softmax_update(h, c, s):
            cols = slice(c * qc, (c + 1) * qc)
            m_prev = m_scr[h, :, cols]
            m_new = jnp.maximum(m_prev, jnp.max(s, axis=0, keepdims=True))
            alpha = jnp.exp2(m_prev - m_new)
            p = jnp.exp2(s - m_new)
            l_scr[h, :, cols] = alpha * l_scr[h, :, cols] + jnp.sum(p, axis=0, keepdims=True)
            m_scr[h, :, cols] = m_new
            return p.astype(BF16), alpha

        def weigh(h, c, p, alpha):
            cols = slice(c * qc, (c + 1) * qc)
            acc_scr[h, :, cols] = alpha * acc_scr[h, :, cols] + jnp.dot(
                vt_ref[j, h * dv:(h + 1) * dv, 0:n_keys(c)], p, preferred_element_type=F32)

        ahead = ATTN_SCORES_AHEAD
        s = {n: scores(*chunks[n]) for n in range(min(ahead, len(chunks)))}
        for n, (h, c) in enumerate(chunks):
            p, alpha = softmax_update(h, c, s.pop(n))
            if n + ahead < len(chunks):
                s[n + ahead] = scores(*chunks[n + ahead])
            weigh(h, c, p, alpha)

    def far_block(j, carry):
        block(j, far)
        return carry

    lax.fori_loop(0, i - 1, far_block, 0)

    @pl.when(i >= 1)
    def _():
        block(i - 1, prev)

    block(i, diag)
    lam = _diff_lambda_value(lam_ref, layer)
    for h in range(heads):
        o = acc_scr[h] / l_scr[h]
        od = o[:, 0:t] - lam * o[:, t:2 * t]
        on = od * lax.rsqrt(jnp.mean(od * od, axis=0, keepdims=True) + SUBLN_EPS)
        o_ref[:, h * dv:(h + 1) * dv] = (on.T * sg_ref[...] * (1.0 - _lambda_init(layer))).astype(BF16)


def _attn_prompt(qt, kb, vt, bias, lam, subln, batch, layer):
    n_tiles, d, tile = qt.shape
    m = kb.shape[0]
    seq = m // batch
    nq = seq // tile
    heads = ATTN_HEADS_PER_STEP
    width = heads * V_HEAD_DIM
    assert d % width == 0
    vmem = (4 * seq * width * 2 + 4 * heads * tile * tile * 4 + 8 * tile * MXU_DIM * 4
            + 4 * width * 2 * tile * 4)
    return pl.pallas_call(
        functools.partial(_attn_prompt_kernel, layer=layer),
        grid=(batch, d // width, nq),
        in_specs=[pl.BlockSpec((1, width, tile), lambda b, g, i: (b * nq + i, g, 0)),
                  pl.BlockSpec((seq, width), lambda b, g, i: (b, g)),
                  pl.BlockSpec((nq, width, tile), lambda b, g, i: (b, g, 0)),
                  pl.BlockSpec((heads, 2, tile, tile), lambda b, g, i: (g, 0, 0, 0)),
                  pl.BlockSpec(lam.shape, lambda b, g, i: (0, 0)),
                  pl.BlockSpec((1, V_HEAD_DIM), lambda b, g, i: (0, 0))],
        out_specs=pl.BlockSpec((tile, width), lambda b, g, i: (b * nq + i, g)),
        out_shape=jax.ShapeDtypeStruct((m, d), BF16),
        scratch_shapes=[pltpu.VMEM((heads, V_HEAD_DIM, 2 * tile), BF16),
                        pltpu.VMEM((heads, 1, 2 * tile), F32),
                        pltpu.VMEM((heads, 1, 2 * tile), F32),
                        pltpu.VMEM((heads, V_HEAD_DIM, 2 * tile), F32)],
        compiler_params=_params(("parallel", "parallel", "arbitrary"), vmem),
        name="attn_prompt",
    )(qt, kb, vt, bias, lam, subln[None, :])


def _attn_sample_kernel(pt_ref, q_ref, kn_ref, vn_ref, *rest, layer, n_pages, dec_seq, slots):
    d = q_ref.shape[2]
    n_heads = d // V_HEAD_DIM
    v_parts = n_heads // VALUE_TILE_HEADS
    kc_refs, vc_refs = rest[:slots], rest[slots:slots + slots * v_parts]
    (blast_ref, bnew_ref, lam_ref, sg_ref, o_ref,
     qbd_scr, lg_scr, w_scr, wn_scr, acc_scr) = rest[slots + slots * v_parts:]
    s = pl.program_id(1)
    k_steps = n_pages // slots
    rows = 2 * n_heads * dec_seq
    half = rows // 2
    page = kc_refs[0].shape[1]
    group_rows = acc_scr.shape[1]
    group_heads = group_rows // dec_seq

    def padded_new(ref):
        x = ref[0]
        return jnp.concatenate([x, jnp.zeros((page - x.shape[0], d), F32)], axis=0).astype(BF16)

    @pl.when(s == 0)
    def _():
        q = q_ref[0]
        qrep = jnp.broadcast_to(q[None], (rows // q.shape[0], q.shape[0], d)).reshape(rows, d)
        ridx = lax.broadcasted_iota(jnp.int32, (rows, d), 0)
        lane = lax.broadcasted_iota(jnp.int32, (rows, d), 1)
        row_head = (ridx % half) // dec_seq
        row_map = ridx // half
        match = (lane // V_HEAD_DIM == row_head) & ((lane // QK_HEAD_DIM) % 2 == row_map)
        qbd_scr[...] = jnp.where(match, qrep, 0.0).astype(BF16)
        acc_scr[...] = jnp.zeros(acc_scr.shape, F32)

    @pl.when(s < k_steps)
    def _():
        for e in range(slots):
            lg_scr[s * slots + e] = jnp.dot(qbd_scr[...], kc_refs[e][...].astype(BF16),
                                            preferred_element_type=F32)

    @pl.when(s == k_steps - 1)
    def _():
        lam = _diff_lambda_value(lam_ref, layer)
        lg_scr[n_pages - 1] = lg_scr[n_pages - 1] + blast_ref[...]
        lgn = lax.dot_general(qbd_scr[...], padded_new(kn_ref), (((1,), (1,)), ((), ())),
                              preferred_element_type=F32) + bnew_ref[...]
        lg = lg_scr[...]
        m = jnp.maximum(jnp.max(jnp.max(lg, axis=0), axis=-1, keepdims=True),
                        jnp.max(lgn, axis=-1, keepdims=True))
        p = jnp.exp(lg - m[None])
        pn = jnp.exp(lgn - m)
        l = jnp.sum(jnp.sum(p, axis=0), axis=-1, keepdims=True) + jnp.sum(pn, axis=-1, keepdims=True)
        inv = 1.0 / l
        p = p * inv[None]
        pn = pn * inv
        w_scr[...] = p[:, 0:half, :] - lam * p[:, half:rows, :]
        wn_scr[...] = (pn[0:half, :] - lam * pn[half:rows, :]).astype(BF16)

    @pl.when(s >= k_steps)
    def _():
        for g in range(n_heads // group_heads):
            part, first = divmod(g * group_heads, VALUE_TILE_HEADS)
            acc = acc_scr[g]
            for e in range(slots):
                vflat = vc_refs[e * v_parts + part].reshape(page * VALUE_TILE_HEADS, V_HEAD_DIM)
                v = jnp.concatenate(
                    [vflat[pl.ds(first + hh, page, stride=VALUE_TILE_HEADS), :].astype(BF16)
                     for hh in range(group_heads)], axis=1)
                w = w_scr[(s - k_steps) * slots + e, g * group_rows:(g + 1) * group_rows, :]
                acc = acc + jnp.dot(w.astype(BF16), v, preferred_element_type=F32)
            acc_scr[g] = acc

    @pl.when(s == 2 * k_steps - 1)
    def _():
        new = jnp.dot(wn_scr[...], padded_new(vn_ref), preferred_element_type=F32)
        scale = 1.0 - _lambda_init(layer)
        for h in range(n_heads):
            g, e = divmod(h, group_heads)
            cols = slice(h * V_HEAD_DIM, (h + 1) * V_HEAD_DIM)
            piece = (acc_scr[g, e * dec_seq:(e + 1) * dec_seq, e * V_HEAD_DIM:(e + 1) * V_HEAD_DIM]
                     + new[h * dec_seq:(h + 1) * dec_seq, cols])
            o_ref[0, :, cols] = _rms_rows(piece, sg_ref[...], SUBLN_EPS) * scale


def _attn_sample(q, k_new, v_new, cache_k, cache_v, page_table, blast, bnew, lam, subln, layer):
    batch, dec_seq, d = q.shape
    n_pages = page_table.shape[1]
    page = cache_k.shape[2]
    n_heads = d // V_HEAD_DIM
    rows = 2 * n_heads * dec_seq
    sub_f32, sub_bf16 = 8, 16
    assert sub_f32 % dec_seq == 0
    q = jnp.tile(q, (1, sub_f32 // dec_seq, 1))
    k_new = jnp.pad(k_new, ((0, 0), (0, sub_bf16 - dec_seq), (0, 0)))
    v_new = jnp.pad(v_new, ((0, 0), (0, sub_bf16 - dec_seq), (0, 0)))
    tok = lambda n: pl.BlockSpec((1, n, d), lambda b, s, pt: (b, 0, 0))
    full = lambda a: pl.BlockSpec(a.shape, lambda b, s, pt: (0,) * a.ndim)
    slots = SAMPLE_PAGE_SLOTS
    assert n_pages % slots == 0
    k_steps = n_pages // slots

    def key_page(e):
        return lambda b, s, pt: (pt[b, jnp.minimum(s, k_steps - 1) * slots + e], 0, 0)

    def value_page(e, part):
        def index(b, s, pt):
            weighting = s >= k_steps
            row = jnp.where(weighting, b, jnp.maximum(b - 1, 0))
            col = jnp.where(weighting, s - k_steps, k_steps - 1) * slots + e
            return (pt[row, col], 0, part, 0)
        return index

    v_parts = n_heads // VALUE_TILE_HEADS
    grid_spec = pltpu.PrefetchScalarGridSpec(
        num_scalar_prefetch=1,
        grid=(batch, 2 * k_steps),
        in_specs=[tok(sub_f32), tok(sub_bf16), tok(sub_bf16)]
                 + [pl.BlockSpec((None, d, page), key_page(e)) for e in range(slots)]
                 + [pl.BlockSpec((None, page, VALUE_TILE_HEADS, V_HEAD_DIM), value_page(e, part))
                    for e in range(slots) for part in range(v_parts)]
                 + [full(blast), full(bnew), full(lam),
                    pl.BlockSpec((1, V_HEAD_DIM), lambda b, s, pt: (0, 0))],
        out_specs=pl.BlockSpec((1, dec_seq, d), lambda b, s, pt: (b, 0, 0)),
        scratch_shapes=[pltpu.VMEM((rows, d), BF16),
                        pltpu.VMEM((n_pages, rows, page), F32),
                        pltpu.VMEM((n_pages, rows // 2, page), F32),
                        pltpu.VMEM((rows // 2, page), BF16),
                        pltpu.VMEM((n_heads * dec_seq // sub_f32, sub_f32,
                                    sub_f32 // dec_seq * V_HEAD_DIM), F32)],
    )
    vmem = 4 * slots * page * d * 4 + 6 * n_pages * rows * page * 4 + 4 * page * d * 2
    return pl.pallas_call(
        functools.partial(_attn_sample_kernel, layer=layer, n_pages=n_pages, dec_seq=dec_seq, slots=slots),
        grid_spec=grid_spec,
        out_shape=jax.ShapeDtypeStruct((batch, dec_seq, d), F32),
        compiler_params=_params(("arbitrary", "arbitrary"), vmem),
        name="attn_sample",
    )(page_table, q, k_new, v_new, *([cache_k] * slots), *([cache_v] * (slots * v_parts)),
      blast, bnew, lam, subln[None, :])


PROMPT_TM = 512
ATTN_TILE = 512
PROJ_TN = 512
RES_TN = 1024
FFN_TF = 1024
FFN_CAST_TF = 512
SAMPLE_PAGE_SLOTS = 8
ATTN_HEADS_PER_STEP = 4
ATTN_SCORES_AHEAD = 4


def kernel(x_prompt, x_sample, cache_k, cache_v, page_table, rel_bias_table, mix_norm, ffn_norm,
           attn_w_qkv, attn_w_o, attn_q_norm, attn_k_norm, attn_lambda, attn_subln,
           cm_w_in, cm_v_norm, cm_w_s, cm_b_s, cm_w_o, ffn_w1, ffn_w2):
    batch, seq, d = x_prompt.shape
    dec_batch, dec_seq, _ = x_sample.shape
    depth = mix_norm.shape[0]
    n_heads = d // V_HEAD_DIM
    n_groups = cm_w_s.shape[1]
    page = cache_k.shape[2]
    mp, ms = batch * seq, dec_batch * dec_seq
    assert seq % ATTN_TILE == 0 and mp % PROMPT_TM == 0 and ms % CHUNK == 0 and CHUNK % dec_seq == 0

    xp = x_prompt.reshape(mp, d)
    xs = x_sample.reshape(ms, d)
    ck = jnp.transpose(cache_k, (0, 1, 3, 4, 5, 2)).reshape(cache_k.shape[0], cache_k.shape[1], d, page)
    cv = cache_v

    bias_p = _prompt_bias(rel_bias_table, n_heads, ATTN_TILE)
    blast, bnew = _sample_bias(rel_bias_table, n_heads, dec_seq, page)

    k_prompt, v_prompt, k_sample, v_sample, cv_sample = [], [], [], [], []
    for i in range(depth):
        j = i // N_MIXERS
        if i % N_MIXERS == 0:
            w_qkv = attn_w_qkv[j].astype(BF16)
            w_o = attn_w_o[j].astype(BF16)
            qtp, kbp, ktp, vp, vtp = _qkv(xp, mix_norm[i], w_qkv, attn_q_norm[j], attn_k_norm[j],
                                          ATTN_TILE, PROJ_TN, seq=seq)
            qs, ks, vs = _qkv(xs, mix_norm[i], w_qkv, attn_q_norm[j], attn_k_norm[j], ms, PROJ_TN)
            op = _attn_prompt(qtp, kbp, vtp, bias_p, attn_lambda[j], attn_subln[j], batch, i)
            os_ = _attn_sample(qs.reshape(dec_batch, dec_seq, d),
                               ks.reshape(dec_batch, dec_seq, d), vs.reshape(dec_batch, dec_seq, d),
                               ck[j], cv[j], page_table, blast, bnew, attn_lambda[j], attn_subln[j], i)
            xp = _mm_res(xp, op, w_o, 2 * PROMPT_TM, RES_TN)
            xs = _mm_res(xs, os_.reshape(ms, d), w_o, ms, RES_TN)
            k_prompt.append(jnp.transpose(ktp.reshape(batch, n_heads, 2, QK_HEAD_DIM, seq), (0, 4, 1, 2, 3)))
            v_prompt.append(vp.reshape(batch, seq, n_heads, V_HEAD_DIM))
            k_sample.append(ks.reshape(dec_batch, dec_seq, n_heads, 2, QK_HEAD_DIM))
            v_sample.append(vs.reshape(dec_batch, dec_seq, n_heads, V_HEAD_DIM))
        else:
            w_in = cm_w_in[j].astype(BF16)
            w_o = cm_w_o[j].astype(BF16)
            bs_p = jnp.broadcast_to(cm_b_s[j][:, :, None], (n_groups, CHUNK, GROUP_DIM))
            gp, _ = _gmlp_front(xp, mix_norm[i], w_in, cm_v_norm[j], cm_w_s[j], bs_p,
                                PROMPT_TM, PROJ_TN, False)
            reps = CHUNK // dec_seq
            eye = jnp.eye(reps, dtype=F32)
            ws_s = (eye[None, :, None, :, None] * cm_w_s[j][:, None, :dec_seq, None, :dec_seq]
                    ).reshape(n_groups, CHUNK, CHUNK)
            bs_s = jnp.broadcast_to(jnp.tile(cm_b_s[j][:, :dec_seq], (1, reps))[:, :, None],
                                    (n_groups, CHUNK, GROUP_DIM))
            gs, cvs = _gmlp_front(xs, mix_norm[i], w_in, cm_v_norm[j], ws_s, bs_s, ms, PROJ_TN, True)
            xp = _mm_res(xp, gp, w_o, 2 * PROMPT_TM, RES_TN)
            xs = _mm_res(xs, gs, w_o, ms, RES_TN)
            cv_sample.append(cvs.reshape(dec_batch, dec_seq, n_groups, GROUP_DIM))
        xs, w1, w2 = _ffn(xs, ffn_norm[i], ffn_w1, ffn_w2, ms, FFN_CAST_TF, layer=i)
        xp = _ffn(xp, ffn_norm[i], w1, w2, PROMPT_TM, FFN_TF)

    return (xp.reshape(batch, seq, d), xs.reshape(dec_batch, dec_seq, d),
            jnp.stack(k_prompt), jnp.stack(v_prompt), jnp.stack(k_sample), jnp.stack(v_sample),
            jnp.stack(cv_sample))
```

```python
import functools
import math

import jax
import jax.numpy as jnp
from jax import lax
from jax.experimental import pallas as pl
from jax.experimental.pallas import tpu as pltpu

F32 = jnp.float32
BF16 = jnp.bfloat16

QK_HEAD_DIM = 64
V_HEAD_DIM = 128
NUM_BUCKETS = 32
MAX_DISTANCE = 128
CHUNK = 128
GROUP_DIM = 128
N_MIXERS = 2
NORM_EPS = 1e-6
SUBLN_EPS = 1e-5
NEG_INF = -1e30
LOG2_E = 1.4426950408889634

LANES = 128
VALUE_TILE_HEADS = 8
MXU_DIM = 256
VMEM_CAP = 56 * 1024 * 1024


def _lambda_init(layer):
    return 0.8 - 0.6 * math.exp(-0.3 * layer)


def _params(semantics, vmem_bytes):
    limit = int(min(VMEM_CAP, max(32 * 1024 * 1024, vmem_bytes * 5 // 4)))
    return pltpu.CompilerParams(dimension_semantics=semantics, vmem_limit_bytes=limit)


def _rms_rows(x, g, eps):
    ms = jnp.mean(x * x, axis=-1, keepdims=True)
    return x * lax.rsqrt(ms + eps) * g


def _group_rms(x, bsum, group, eps):
    x2 = x * x
    hi = x2.astype(BF16)
    lo = (x2 - hi.astype(F32)).astype(BF16)
    parts = []
    for c in range(x.shape[1] // MXU_DIM):
        sl = slice(c * MXU_DIM, (c + 1) * MXU_DIM)
        parts.append(jnp.dot(hi[:, sl], bsum, preferred_element_type=F32)
                     + jnp.dot(lo[:, sl], bsum, preferred_element_type=F32))
    ss = parts[0] if len(parts) == 1 else jnp.concatenate(parts, axis=1)
    return x * lax.rsqrt(ss * (1.0 / group) + eps)


def _group_sum_matrix(group):
    r = jnp.arange(MXU_DIM) // group
    return (r[:, None] == r[None, :]).astype(BF16)


def _diff_lambda_value(lam_ref, layer):
    lf = lam_ref[...]
    a = jnp.sum(lf[0:1] * lf[1:2], axis=-1, keepdims=True)
    b = jnp.sum(lf[2:3] * lf[3:4], axis=-1, keepdims=True)
    return jnp.exp(a) - jnp.exp(b) + _lambda_init(layer)


def _qkv_kernel(x_ref, g_ref, wq_ref, wk_ref, wv_ref, qg_ref, kg_ref, bsum_ref, *rest, feature_major):
    if feature_major:
        qt_ref, kb_ref, kt_ref, v_ref, vt_ref, h_scr = rest
    else:
        q_ref, k_ref, v_ref, h_scr = rest

    @pl.when(pl.program_id(1) == 0)
    def _():
        h_scr[...] = _rms_rows(x_ref[...], g_ref[...], NORM_EPS).astype(BF16)

    h = h_scr[...]
    bsum = bsum_ref[...]
    q_scale = QK_HEAD_DIM ** -0.5 * (LOG2_E if feature_major else 1.0)
    q = jnp.dot(h, wq_ref[...], preferred_element_type=F32)
    qn = _group_rms(q, bsum, QK_HEAD_DIM, NORM_EPS) * qg_ref[...] * q_scale
    k = jnp.dot(h, wk_ref[...], preferred_element_type=F32)
    kn = _group_rms(k, bsum, QK_HEAD_DIM, NORM_EPS) * kg_ref[...]
    v = jnp.dot(h, wv_ref[...], preferred_element_type=F32)
    v_ref[...] = v
    if feature_major:
        qt_ref[0] = qn.T.astype(BF16)
        kb_ref[...] = kn.astype(BF16)
        kt_ref[0] = kn.T
        vt_ref[0] = v.T.astype(BF16)
    else:
        q_ref[...] = qn
        k_ref[...] = kn


def _qkv(x, g, w_qkv, q_gain, k_gain, tm, tn, seq=None):
    m, d = x.shape
    n = w_qkv.shape[1] // 3
    nj = n // tn
    qg = jnp.tile(q_gain, tn // QK_HEAD_DIM)[None, :]
    kg = jnp.tile(k_gain, tn // QK_HEAD_DIM)[None, :]
    bsum = _group_sum_matrix(QK_HEAD_DIM)
    row = pl.BlockSpec((tm, d), lambda i, j: (i, 0))
    col = lambda off: pl.BlockSpec((d, tn), lambda i, j, off=off: (0, j + off))
    vec = lambda width: pl.BlockSpec((1, width), lambda i, j: (0, 0))
    out = pl.BlockSpec((tm, tn), lambda i, j: (i, j))
    if seq is None:
        out_specs = [out, out, out]
        out_shape = [jax.ShapeDtypeStruct((m, n), F32)] * 3
    else:
        per_seq = seq // tm
        tile_t = pl.BlockSpec((1, tn, tm), lambda i, j: (i, j, 0))
        seq_t = pl.BlockSpec((1, tn, tm), lambda i, j: (i // per_seq, j, i % per_seq))
        out_specs = [tile_t, out, seq_t, out, tile_t]
        out_shape = [jax.ShapeDtypeStruct((m // tm, n, tm), BF16), jax.ShapeDtypeStruct((m, n), BF16),
                     jax.ShapeDtypeStruct((m // seq, n, seq), F32), jax.ShapeDtypeStruct((m, n), F32),
                     jax.ShapeDtypeStruct((m // tm, n, tm), BF16)]
    vmem = 2 * tm * d * 4 + tm * d * 2 + 6 * d * tn * 2 + 2 * tm * tn * 16 + 10 * tm * tn * 4
    return pl.pallas_call(
        functools.partial(_qkv_kernel, feature_major=seq is not None),
        grid=(m // tm, nj),
        in_specs=[row, vec(d), col(0), col(nj), col(2 * nj), vec(tn), vec(tn),
                  pl.BlockSpec((MXU_DIM, MXU_DIM), lambda i, j: (0, 0))],
        out_specs=out_specs,
        out_shape=out_shape,
        scratch_shapes=[pltpu.VMEM((tm, d), BF16)],
        compiler_params=_params(("parallel", "arbitrary"), vmem),
        name="qkv",
    )(x, g[None, :], w_qkv, w_qkv, w_qkv, qg, kg, bsum)


def _mm_res_kernel(x_ref, a_ref, w_ref, o_ref):
    o_ref[...] = x_ref[...] + jnp.dot(a_ref[...].astype(BF16), w_ref[...],
                                      preferred_element_type=F32)


def _mm_res(x, a, w, tm, tn):
    m, n = x.shape
    kd = a.shape[1]
    vmem = 2 * tm * kd * a.dtype.itemsize + 2 * kd * tn * 2 + 4 * tm * tn * 4 + tm * kd * 2 + tm * tn * 4
    return pl.pallas_call(
        _mm_res_kernel,
        grid=(m // tm, n // tn),
        in_specs=[pl.BlockSpec((tm, tn), lambda i, j: (i, j)),
                  pl.BlockSpec((tm, kd), lambda i, j: (i, 0)),
                  pl.BlockSpec((kd, tn), lambda i, j: (0, j))],
        out_specs=pl.BlockSpec((tm, tn), lambda i, j: (i, j)),
        out_shape=jax.ShapeDtypeStruct((m, n), F32),
        compiler_params=_params(("parallel", "arbitrary"), vmem),
        name="mm_res",
    )(x, a, w)


def _ffn_kernel(x_ref, g_ref, w1_ref, w2_ref, o_ref, *rest, emit_weights):
    if emit_weights:
        w1b_ref, w2b_ref, h_scr = rest
    else:
        (h_scr,) = rest

    @pl.when(pl.program_id(1) == 0)
    def _():
        x = x_ref[...]
        h_scr[...] = _rms_rows(x, g_ref[...], NORM_EPS).astype(BF16)
        o_ref[...] = x

    w1 = w1_ref[...].astype(BF16)
    w2 = w2_ref[...].astype(BF16)
    if emit_weights:
        w1b_ref[...] = w1
        w2b_ref[...] = w2
    a = jnp.dot(h_scr[...], w1, preferred_element_type=F32)
    a = jnp.maximum(a, 0.0)
    a = (a * a).astype(BF16)
    o_ref[...] += jnp.dot(a, w2, preferred_element_type=F32)


def _ffn(x, g, w1, w2, tm, tf, layer=None):
    m, d = x.shape
    dff = w1.shape[-1]
    emit = layer is not None
    assert not emit or m == tm
    if emit:
        w1_spec = pl.BlockSpec((None, d, tf), lambda i, f: (layer, 0, f))
        w2_spec = pl.BlockSpec((None, tf, d), lambda i, f: (layer, f, 0))
    else:
        w1_spec = pl.BlockSpec((d, tf), lambda i, f: (0, f))
        w2_spec = pl.BlockSpec((tf, d), lambda i, f: (f, 0))
    out_specs = [pl.BlockSpec((tm, d), lambda i, f: (i, 0))]
    out_shape = [jax.ShapeDtypeStruct((m, d), F32)]
    if emit:
        out_specs += [pl.BlockSpec((d, tf), lambda i, f: (0, f)), pl.BlockSpec((tf, d), lambda i, f: (f, 0))]
        out_shape += [jax.ShapeDtypeStruct((d, dff), BF16), jax.ShapeDtypeStruct((dff, d), BF16)]
    vmem = (4 * tm * d * 4 + tm * d * 2 + 4 * d * tf * w1.dtype.itemsize + tm * tf * 8
            + (6 * d * tf * 2 if emit else 0))
    res = pl.pallas_call(
        functools.partial(_ffn_kernel, emit_weights=emit),
        grid=(m // tm, dff // tf),
        in_specs=[pl.BlockSpec((tm, d), lambda i, f: (i, 0)),
                  pl.BlockSpec((1, d), lambda i, f: (0, 0)),
                  w1_spec, w2_spec],
        out_specs=out_specs,
        out_shape=out_shape,
        scratch_shapes=[pltpu.VMEM((tm, d), BF16)],
        compiler_params=_params(("parallel", "arbitrary"), vmem),
        name="ffn",
    )(x, g[None, :], w1, w2)
    return res if emit else res[0]


def _gelu(x):
    return 0.5 * x * (1.0 + lax.erf(x * (2.0 ** -0.5)))


def _gmlp_kernel(x_ref, g_ref, wu_ref, wv_ref, vg_ref, bsum_ref, ws_ref, bs_ref, *rest, emit_cv):
    if emit_cv:
        gate_ref, cv_ref, h_scr = rest
    else:
        gate_ref, h_scr = rest

    @pl.when(pl.program_id(1) == 0)
    def _():
        h_scr[...] = _rms_rows(x_ref[...], g_ref[...], NORM_EPS).astype(BF16)

    h = h_scr[...]
    tm = h.shape[0]
    tn = wu_ref.shape[1]
    u = _gelu(jnp.dot(h, wu_ref[...], preferred_element_type=F32))
    v = _gelu(jnp.dot(h, wv_ref[...], preferred_element_type=F32))
    vn = _group_rms(v, bsum_ref[...], GROUP_DIM, NORM_EPS) * vg_ref[...]
    if emit_cv:
        cv_ref[...] = vn
    vb = vn.astype(BF16)
    t_idx = lax.broadcasted_iota(jnp.int32, (CHUNK, CHUNK), 0)
    s_idx = lax.broadcasted_iota(jnp.int32, (CHUNK, CHUNK), 1)
    causal = t_idx >= s_idx
    for gi in range(tn // GROUP_DIM):
        cols = slice(gi * GROUP_DIM, (gi + 1) * GROUP_DIM)
        w = jnp.where(causal, ws_ref[gi], 0.0).astype(BF16)
        bias = bs_ref[gi]
        for c in range(tm // CHUNK):
            rows = slice(c * CHUNK, (c + 1) * CHUNK)
            mixed = jnp.dot(w, vb[rows, cols], preferred_element_type=F32) + bias
            gate_ref[rows, cols] = (u[rows, cols] * mixed).astype(BF16)


def _gmlp_front(x, g, w_in, v_gain, ws, bs, tm, tn, emit_cv):
    m, d = x.shape
    dg = w_in.shape[1] // 2
    nj = dg // tn
    gpt = tn // GROUP_DIM
    blk = pl.BlockSpec((tm, tn), lambda i, j: (i, j))
    out_specs = [blk, blk] if emit_cv else [blk]
    out_shape = [jax.ShapeDtypeStruct((m, dg), BF16)]
    if emit_cv:
        out_shape.append(jax.ShapeDtypeStruct((m, dg), F32))
    vmem = 2 * tm * d * 4 + tm * d * 2 + 4 * d * tn * 2 + 4 * gpt * CHUNK * CHUNK * 4 + 12 * tm * tn * 4
    res = pl.pallas_call(
        functools.partial(_gmlp_kernel, emit_cv=emit_cv),
        grid=(m // tm, nj),
        in_specs=[pl.BlockSpec((tm, d), lambda i, j: (i, 0)),
                  pl.BlockSpec((1, d), lambda i, j: (0, 0)),
                  pl.BlockSpec((d, tn), lambda i, j: (0, j)),
                  pl.BlockSpec((d, tn), lambda i, j, nj=nj: (0, j + nj)),
                  pl.BlockSpec((1, tn), lambda i, j: (0, j)),
                  pl.BlockSpec((MXU_DIM, MXU_DIM), lambda i, j: (0, 0)),
                  pl.BlockSpec((gpt, CHUNK, CHUNK), lambda i, j: (j, 0, 0)),
                  pl.BlockSpec((gpt, CHUNK, GROUP_DIM), lambda i, j: (j, 0, 0))],
        out_specs=out_specs,
        out_shape=out_shape,
        scratch_shapes=[pltpu.VMEM((tm, d), BF16)],
        compiler_params=_params(("parallel", "arbitrary"), vmem),
        name="gmlp_front",
    )(x, g[None, :], w_in, w_in, v_gain[None, :], _group_sum_matrix(GROUP_DIM), ws, bs)
    return res if emit_cv else (res[0], None)


def _shifted_bias(dist, table_ref, head):
    max_exact = NUM_BUCKETS // 2
    n = jnp.maximum(dist, 0)
    nf = jnp.maximum(n, 1).astype(F32)
    large = max_exact + (jnp.log(nf / max_exact) / math.log(MAX_DISTANCE / max_exact)
                         * (NUM_BUCKETS - max_exact)).astype(jnp.int32)
    bucket = jnp.where(n < max_exact, n, jnp.minimum(large, NUM_BUCKETS - 1))
    far = table_ref[NUM_BUCKETS - 1, head]
    acc = jnp.zeros(dist.shape, F32)
    for b in range(NUM_BUCKETS - 1):
        acc = jnp.where(bucket == b, table_ref[b, head] - far, acc)
    return acc


def _prompt_bias_kernel(table_ref, o_ref, *, tile):
    head = pl.program_id(0)
    key = lax.broadcasted_iota(jnp.int32, (LANES, LANES), 0)
    qry = lax.broadcasted_iota(jnp.int32, (LANES, LANES), 1)
    d0 = jnp.where(qry >= key, _shifted_bias(qry - key, table_ref, head) * LOG2_E, NEG_INF)
    d1 = _shifted_bias(LANES + qry - key, table_ref, head) * LOG2_E
    nb = tile // LANES
    zeros = jnp.zeros((LANES, LANES), F32)
    masked = jnp.full((LANES, LANES), NEG_INF, F32)
    for a in range(nb):
        for b in range(nb):
            rows = slice(a * LANES, (a + 1) * LANES)
            cols = slice(b * LANES, (b + 1) * LANES)
            diag = d0 if b == a else d1 if b == a + 1 else zeros if b > a else masked
            o_ref[0, 0, rows, cols] = diag
            o_ref[0, 1, rows, cols] = d1 if (a == nb - 1 and b == 0) else zeros


def _prompt_bias(table, n_heads, tile):
    return pl.pallas_call(
        functools.partial(_prompt_bias_kernel, tile=tile),
        grid=(n_heads,),
        in_specs=[pl.BlockSpec(memory_space=pltpu.SMEM)],
        out_specs=pl.BlockSpec((1, 2, tile, tile), lambda h: (h, 0, 0, 0)),
        out_shape=jax.ShapeDtypeStruct((n_heads, 2, tile, tile), F32),
        compiler_params=_params(("parallel",), 8 * tile * tile * 4),
        name="prompt_bias",
    )(table)


def _sample_bias_kernel(table_ref, last_ref, new_ref, *, n_heads, dec_seq, page):
    rows = 2 * n_heads * dec_seq
    ridx = lax.broadcasted_iota(jnp.int32, (rows, page), 0)
    key = lax.broadcasted_iota(jnp.int32, (rows, page), 1)
    tok = ridx % dec_seq
    row_head = (ridx % (n_heads * dec_seq)) // dec_seq
    last = jnp.zeros((rows, page), F32)
    new = jnp.zeros((rows, page), F32)
    for h in range(n_heads):
        last = jnp.where(row_head == h, _shifted_bias(page + tok - key, table_ref, h), last)
        new = jnp.where(row_head == h, _shifted_bias(tok - key, table_ref, h), new)
    last_ref[...] = last
    new_ref[...] = jnp.where(key <= tok, new, NEG_INF)


def _sample_bias(table, n_heads, dec_seq, page):
    rows = 2 * n_heads * dec_seq
    return pl.pallas_call(
        functools.partial(_sample_bias_kernel, n_heads=n_heads, dec_seq=dec_seq, page=page),
        in_specs=[pl.BlockSpec(memory_space=pltpu.SMEM)],
        out_specs=[pl.BlockSpec(memory_space=pltpu.VMEM), pl.BlockSpec(memory_space=pltpu.VMEM)],
        out_shape=[jax.ShapeDtypeStruct((rows, page), F32), jax.ShapeDtypeStruct((rows, page), F32)],
        name="sample_bias",
    )(table)


def _attn_prompt_kernel(qt_ref, k_ref, vt_ref, bias_ref, lam_ref, sg_ref, o_ref,
                        qs_scr, m_scr, l_scr, acc_scr, *, layer):
    i = pl.program_id(2)
    t = qt_ref.shape[2]
    hd = QK_HEAD_DIM
    dv = V_HEAD_DIM
    heads = qs_scr.shape[0]
    zero = jnp.zeros((hd, t), BF16)
    for h in range(heads):
        qs_scr[h, 0:hd, 0:t] = qt_ref[0, h * dv:h * dv + hd, :]
        qs_scr[h, hd:2 * hd, 0:t] = zero
        qs_scr[h, 0:hd, t:2 * t] = zero
        qs_scr[h, hd:2 * hd, t:2 * t] = qt_ref[0, h * dv + hd:(h + 1) * dv, :]
    m_scr[...] = jnp.full(m_scr.shape, NEG_INF, F32)
    l_scr[...] = jnp.zeros(l_scr.shape, F32)
    acc_scr[...] = jnp.zeros(acc_scr.shape, F32)

    far, prev, diag = 0, 1, 2
    qc = MXU_DIM
    per_map = t // qc
    chunks = [(h, c) for c in range(2 * per_map) for h in range(heads)]

    def block(j, kind):
        base = pl.multiple_of(j * t, t)

        def n_keys(c):
            return (c % per_map + 1) * qc if kind == diag else t

        def scores(h, c):
            nk, cq = n_keys(c), c % per_map
            k = k_ref[pl.ds(base, nk), h * dv:(h + 1) * dv]
            s = jnp.dot(k, qs_scr[h, :, c * qc:(c + 1) * qc], preferred_element_type=F32)
            if kind == diag:
                s = s + bias_ref[h, 0, 0:nk, cq * qc:(cq + 1) * qc]
            elif kind == prev and cq == 0:
                s = s + bias_ref[h, 1, :, 0:qc]
            return s

        def softmax_update(h, c, s):
            cols = slice(c * qc, (c + 1) * qc)
            m_prev = m_scr[h, :, cols]
            m_new = jnp.maximum(m_prev, jnp.max(s, axis=0, keepdims=True))
            alpha = jnp.exp2(m_prev - m_new)
            p = jnp.exp2(s - m_new)
            l_scr[h, :, cols] = alpha * l_scr[h, :, cols] + jnp.sum(p, axis=0, keepdims=True)
            m_scr[h, :, cols] = m_new
            return p.astype(BF16), alpha

        def weigh(h, c, p, alpha):
            cols = slice(c * qc, (c + 1) * qc)
            acc_scr[h, :, cols] = alpha * acc_scr[h, :, cols] + jnp.dot(
                vt_ref[j, h * dv:(h + 1) * dv, 0:n_keys(c)], p, preferred_element_type=F32)

        ahead = ATTN_SCORES_AHEAD
        s = {n: scores(*chunks[n]) for n in range(min(ahead, len(chunks)))}
        for n, (h, c) in enumerate(chunks):
            p, alpha = softmax_update(h, c, s.pop(n))
            if n + ahead < len(chunks):
                s[n + ahead] = scores(*chunks[n + ahead])
            weigh(h, c, p, alpha)

    def far_block(j, carry):
        block(j, far)
        return carry

    lax.fori_loop(0, i - 1, far_block, 0)

    @pl.when(i >= 1)
    def _():
        block(i - 1, prev)

    block(i, diag)
    lam = _diff_lambda_value(lam_ref, layer)
    for h in range(heads):
        o = acc_scr[h] / l_scr[h]
        od = o[:, 0:t] - lam * o[:, t:2 * t]
        on = od * lax.rsqrt(jnp.mean(od * od, axis=0, keepdims=True) + SUBLN_EPS)
        o_ref[:, h * dv:(h + 1) * dv] = (on.T * sg_ref[...] * (1.0 - _lambda_init(layer))).astype(BF16)


def _attn_prompt(qt, kb, vt, bias, lam, subln, batch, layer):
    n_tiles, d, tile = qt.shape
    m = kb.shape[0]
    seq = m // batch
    nq = seq // tile
    heads = ATTN_HEADS_PER_STEP
    width = heads * V_HEAD_DIM
    assert d % width == 0
    vmem = (4 * seq * width * 2 + 4 * heads * tile * tile * 4 + 8 * tile * MXU_DIM * 4
            + 4 * width * 2 * tile * 4)
    return pl.pallas_call(
        functools.partial(_attn_prompt_kernel, layer=layer),
        grid=(batch, d // width, nq),
        in_specs=[pl.BlockSpec((1, width, tile), lambda b, g, i: (b * nq + i, g, 0)),
                  pl.BlockSpec((seq, width), lambda b, g, i: (b, g)),
                  pl.BlockSpec((nq, width, tile), lambda b, g, i: (b, g, 0)),
                  pl.BlockSpec((heads, 2, tile, tile), lambda b, g, i: (g, 0, 0, 0)),
                  pl.BlockSpec(lam.shape, lambda b, g, i: (0, 0)),
                  pl.BlockSpec((1, V_HEAD_DIM), lambda b, g, i: (0, 0))],
        out_specs=pl.BlockSpec((tile, width), lambda b, g, i: (b * nq + i, g)),
        out_shape=jax.ShapeDtypeStruct((m, d), BF16),
        scratch_shapes=[pltpu.VMEM((heads, V_HEAD_DIM, 2 * tile), BF16),
                        pltpu.VMEM((heads, 1, 2 * tile), F32),
                        pltpu.VMEM((heads, 1, 2 * tile), F32),
                        pltpu.VMEM((heads, V_HEAD_DIM, 2 * tile), F32)],
        compiler_params=_params(("parallel", "parallel", "arbitrary"), vmem),
        name="attn_prompt",
    )(qt, kb, vt, bias, lam, subln[None, :])


def _attn_sample_kernel(pt_ref, q_ref, kn_ref, vn_ref, kc_hbm, vc_hbm, blast_ref, bnew_ref, lam_ref,
                        sg_ref, o_ref, qbd_scr, lg_scr, w_scr, wn_scr, acc_scr, kbuf, vbuf, ksem, vsem,
                        *, layer, n_pages, dec_seq, slots):
    b = pl.program_id(0)
    s = pl.program_id(1)
    d = q_ref.shape[2]
    n_heads = d // V_HEAD_DIM
    v_parts = n_heads // VALUE_TILE_HEADS
    k_steps = n_pages // slots
    rows = 2 * n_heads * dec_seq
    half = rows // 2
    page = kbuf.shape[3]
    group_rows = acc_scr.shape[1]
    group_heads = group_rows // dec_seq

    def key_copies(bb, grp):
        return [pltpu.make_async_copy(kc_hbm.at[pt_ref[bb, grp * slots + e]], kbuf.at[grp, e], ksem.at[grp])
                for e in range(slots)]

    def value_copies(bb, grp):
        return [pltpu.make_async_copy(
                    vc_hbm.at[pt_ref[bb, grp * slots + e], :, pl.ds(part * VALUE_TILE_HEADS, VALUE_TILE_HEADS), :],
                    vbuf.at[grp, e, part], vsem.at[grp])
                for e in range(slots) for part in range(v_parts)]

    @pl.when((b == 0) & (s == 0))
    def _():
        for grp in range(k_steps):
            for cp in key_copies(0, grp):
                cp.start()

    @pl.when(s < k_steps)
    def _():
        for cp in value_copies(b, s):
            cp.start()

    @pl.when((s >= k_steps) & (b + 1 < pl.num_programs(0)))
    def _():
        for cp in key_copies(b + 1, s - k_steps):
            cp.start()

    def padded_new(ref):
        x = ref[0]
        return jnp.concatenate([x, jnp.zeros((page - x.shape[0], d), F32)], axis=0).astype(BF16)

    @pl.when(s == 0)
    def _():
        q = q_ref[0]
        qrep = jnp.broadcast_to(q[None], (rows // q.shape[0], q.shape[0], d)).reshape(rows, d)
        ridx = lax.broadcasted_iota(jnp.int32, (rows, d), 0)
        lane = lax.broadcasted_iota(jnp.int32, (rows, d), 1)
        row_head = (ridx % half) // dec_seq
        row_map = ridx // half
        match = (lane // V_HEAD_DIM == row_head) & ((lane // QK_HEAD_DIM) % 2 == row_map)
        qbd_scr[...] = jnp.where(match, qrep, 0.0).astype(BF16)
        acc_scr[...] = jnp.zeros(acc_scr.shape, F32)

    @pl.when(s < k_steps)
    def _():
        for cp in key_copies(b, s):
            cp.wait()
        for e in range(slots):
            lg_scr[s * slots + e] = jnp.dot(qbd_scr[...], kbuf[s, e].astype(BF16),
                                            preferred_element_type=F32)

    @pl.when(s == k_steps - 1)
    def _():
        lam = _diff_lambda_value(lam_ref, layer)
        lg_scr[n_pages - 1] = lg_scr[n_pages - 1] + blast_ref[...]
        lgn = lax.dot_general(qbd_scr[...], padded_new(kn_ref), (((1,), (1,)), ((), ())),
                              preferred_element_type=F32) + bnew_ref[...]
        lg = lg_scr[...]
        m = jnp.maximum(jnp.max(jnp.max(lg, axis=0), axis=-1, keepdims=True),
                        jnp.max(lgn, axis=-1, keepdims=True))
        p = jnp.exp(lg - m[None])
        pn = jnp.exp(lgn - m)
        l = jnp.sum(jnp.sum(p, axis=0), axis=-1, keepdims=True) + jnp.sum(pn, axis=-1, keepdims=True)
        inv = 1.0 / l
        p = p * inv[None]
        pn = pn * inv
        w_scr[...] = p[:, 0:half, :] - lam * p[:, half:rows, :]
        wn_scr[...] = (pn[0:half, :] - lam * pn[half:rows, :]).astype(BF16)

    @pl.when(s >= k_steps)
    def _():
        grp = s - k_steps
        for cp in value_copies(b, grp):
            cp.wait()
        for g in range(n_heads // group_heads):
            part, first = divmod(g * group_heads, VALUE_TILE_HEADS)
            acc = acc_scr[g]
            for e in range(slots):
                vflat = vbuf.at[grp, e, part].reshape(page * VALUE_TILE_HEADS, V_HEAD_DIM)
                v = jnp.concatenate(
                    [vflat[pl.ds(first + hh, page, stride=VALUE_TILE_HEADS), :].astype(BF16)
                     for hh in range(group_heads)], axis=1)
                w = w_scr[grp * slots + e, g * group_rows:(g + 1) * group_rows, :]
                acc = acc + jnp.dot(w.astype(BF16), v, preferred_element_type=F32)
            acc_scr[g] = acc

    @pl.when(s == 2 * k_steps - 1)
    def _():
        new = jnp.dot(wn_scr[...], padded_new(vn_ref), preferred_element_type=F32)
        scale = 1.0 - _lambda_init(layer)
        for h in range(n_heads):
            g, e = divmod(h, group_heads)
            cols = slice(h * V_HEAD_DIM, (h + 1) * V_HEAD_DIM)
            piece = (acc_scr[g, e * dec_seq:(e + 1) * dec_seq, e * V_HEAD_DIM:(e + 1) * V_HEAD_DIM]
                     + new[h * dec_seq:(h + 1) * dec_seq, cols])
            o_ref[0, :, cols] = _rms_rows(piece, sg_ref[...], SUBLN_EPS) * scale


def _attn_sample(q, k_new, v_new, cache_k, cache_v, page_table, blast, bnew, lam, subln, layer):
    batch, dec_seq, d = q.shape
    n_pages = page_table.shape[1]
    page = cache_k.shape[2]
    n_heads = d // V_HEAD_DIM
    rows = 2 * n_heads * dec_seq
    sub_f32, sub_bf16 = 8, 16
    assert sub_f32 % dec_seq == 0
    q = jnp.tile(q, (1, sub_f32 // dec_seq, 1))
    k_new = jnp.pad(k_new, ((0, 0), (0, sub_bf16 - dec_seq), (0, 0)))
    v_new = jnp.pad(v_new, ((0, 0), (0, sub_bf16 - dec_seq), (0, 0)))
    tok = lambda n: pl.BlockSpec((1, n, d), lambda b, s, pt: (b, 0, 0))
    full = lambda a: pl.BlockSpec(a.shape, lambda b, s, pt: (0,) * a.ndim)
    slots = SAMPLE_PAGE_SLOTS
    assert n_pages % slots == 0
    k_steps = n_pages // slots
    v_parts = n_heads // VALUE_TILE_HEADS
    grid_spec = pltpu.PrefetchScalarGridSpec(
        num_scalar_prefetch=1,
        grid=(batch, 2 * k_steps),
        in_specs=[tok(sub_f32), tok(sub_bf16), tok(sub_bf16),
                  pl.BlockSpec(memory_space=pl.ANY), pl.BlockSpec(memory_space=pl.ANY),
                  full(blast), full(bnew), full(lam),
                  pl.BlockSpec((1, V_HEAD_DIM), lambda b, s, pt: (0, 0))],
        out_specs=pl.BlockSpec((1, dec_seq, d), lambda b, s, pt: (b, 0, 0)),
        scratch_shapes=[pltpu.VMEM((rows, d), BF16),
                        pltpu.VMEM((n_pages, rows, page), F32),
                        pltpu.VMEM((n_pages, rows // 2, page), F32),
                        pltpu.VMEM((rows // 2, page), BF16),
                        pltpu.VMEM((n_heads * dec_seq // sub_f32, sub_f32,
                                    sub_f32 // dec_seq * V_HEAD_DIM), F32),
                        pltpu.VMEM((k_steps, slots, d, page), F32),
                        pltpu.VMEM((k_steps, slots, v_parts, page, VALUE_TILE_HEADS, V_HEAD_DIM), F32),
                        pltpu.SemaphoreType.DMA((k_steps,)),
                        pltpu.SemaphoreType.DMA((k_steps,))],
    )
    vmem = 2 * k_steps * slots * page * d * 4 + 6 * n_pages * rows * page * 4 + 4 * page * d * 2
    return pl.pallas_call(
        functools.partial(_attn_sample_kernel, layer=layer, n_pages=n_pages, dec_seq=dec_seq, slots=slots),
        grid_spec=grid_spec,
        out_shape=jax.ShapeDtypeStruct((batch, dec_seq, d), F32),
        compiler_params=_params(("arbitrary", "arbitrary"), vmem),
        name="attn_sample",
    )(page_table, q, k_new, v_new, cache_k, cache_v, blast, bnew, lam, subln[None, :])


PROMPT_TM = 512
ATTN_TILE = 512
PROJ_TN = 512
RES_TN = 2048
FFN_TF = 1024
FFN_CAST_TF = 512
SAMPLE_PAGE_SLOTS = 8
ATTN_HEADS_PER_STEP = 4
ATTN_SCORES_AHEAD = 4


def kernel(x_prompt, x_sample, cache_k, cache_v, page_table, rel_bias_table, mix_norm, ffn_norm,
           attn_w_qkv, attn_w_o, attn_q_norm, attn_k_norm, attn_lambda, attn_subln,
           cm_w_in, cm_v_norm, cm_w_s, cm_b_s, cm_w_o, ffn_w1, ffn_w2):
    batch, seq, d = x_prompt.shape
    dec_batch, dec_seq, _ = x_sample.shape
    depth = mix_norm.shape[0]
    n_heads = d // V_HEAD_DIM
    n_groups = cm_w_s.shape[1]
    page = cache_k.shape[2]
    mp, ms = batch * seq, dec_batch * dec_seq
    assert seq % ATTN_TILE == 0 and mp % PROMPT_TM == 0 and ms % CHUNK == 0 and CHUNK % dec_seq == 0

    xp = x_prompt.reshape(mp, d)
    xs = x_sample.reshape(ms, d)
    ck = jnp.transpose(cache_k, (0, 1, 3, 4, 5, 2)).reshape(cache_k.shape[0], cache_k.shape[1], d, page)
    cv = cache_v

    bias_p = _prompt_bias(rel_bias_table, n_heads, ATTN_TILE)
    blast, bnew = _sample_bias(rel_bias_table, n_heads, dec_seq, page)

    k_prompt, v_prompt, k_sample, v_sample, cv_sample = [], [], [], [], []
    for i in range(depth):
        j = i // N_MIXERS
        if i % N_MIXERS == 0:
            w_qkv = attn_w_qkv[j].astype(BF16)
            w_o = attn_w_o[j].astype(BF16)
            qtp, kbp, ktp, vp, vtp = _qkv(xp, mix_norm[i], w_qkv, attn_q_norm[j], attn_k_norm[j],
                                          ATTN_TILE, PROJ_TN, seq=seq)
            qs, ks, vs = _qkv(xs, mix_norm[i], w_qkv, attn_q_norm[j], attn_k_norm[j], ms, PROJ_TN)
            op = _attn_prompt(qtp, kbp, vtp, bias_p, attn_lambda[j], attn_subln[j], batch, i)
            os_ = _attn_sample(qs.reshape(dec_batch, dec_seq, d),
                               ks.reshape(dec_batch, dec_seq, d), vs.reshape(dec_batch, dec_seq, d),
                               ck[j], cv[j], page_table, blast, bnew, attn_lambda[j], attn_subln[j], i)
            xp = _mm_res(xp, op, w_o, PROMPT_TM, RES_TN)
            xs = _mm_res(xs, os_.reshape(ms, d), w_o, ms, RES_TN)
            k_prompt.append(jnp.transpose(ktp.reshape(batch, n_heads, 2, QK_HEAD_DIM, seq), (0, 4, 1, 2, 3)))
            v_prompt.append(vp.reshape(batch, seq, n_heads, V_HEAD_DIM))
            k_sample.append(ks.reshape(dec_batch, dec_seq, n_heads, 2, QK_HEAD_DIM))
            v_sample.append(vs.reshape(dec_batch, dec_seq, n_heads, V_HEAD_DIM))
        else:
            w_in = cm_w_in[j].astype(BF16)
            w_o = cm_w_o[j].astype(BF16)
            bs_p = jnp.broadcast_to(cm_b_s[j][:, :, None], (n_groups, CHUNK, GROUP_DIM))
            gp, _ = _gmlp_front(xp, mix_norm[i], w_in, cm_v_norm[j], cm_w_s[j], bs_p,
                                PROMPT_TM, PROJ_TN, False)
            reps = CHUNK // dec_seq
            eye = jnp.eye(reps, dtype=F32)
            ws_s = (eye[None, :, None, :, None] * cm_w_s[j][:, None, :dec_seq, None, :dec_seq]
                    ).reshape(n_groups, CHUNK, CHUNK)
            bs_s = jnp.broadcast_to(jnp.tile(cm_b_s[j][:, :dec_seq], (1, reps))[:, :, None],
                                    (n_groups, CHUNK, GROUP_DIM))
            gs, cvs = _gmlp_front(xs, mix_norm[i], w_in, cm_v_norm[j], ws_s, bs_s, ms, PROJ_TN, True)
            xp = _mm_res(xp, gp, w_o, PROMPT_TM, RES_TN)
            xs = _mm_res(xs, gs, w_o, ms, RES_TN)
            cv_sample.append(cvs.reshape(dec_batch, dec_seq, n_groups, GROUP_DIM))
        xs, w1, w2 = _ffn(xs, ffn_norm[i], ffn_w1, ffn_w2, ms, FFN_CAST_TF, layer=i)
        xp = _ffn(xp, ffn_norm[i], w1, w2, PROMPT_TM, FFN_TF)

    return (xp.reshape(batch, seq, d), xs.reshape(dec_batch, dec_seq, d),
            jnp.stack(k_prompt), jnp.stack(v_prompt), jnp.stack(k_sample), jnp.stack(v_sample),
            jnp.stack(cv_sample))
```

```python
import functools
import math

import jax
import jax.numpy as jnp
from jax import lax
from jax.experimental import pallas as pl
from jax.experimental.pallas import tpu as pltpu

F32 = jnp.float32
BF16 = jnp.bfloat16

QK_HEAD_DIM = 64
V_HEAD_DIM = 128
NUM_BUCKETS = 32
MAX_DISTANCE = 128
CHUNK = 128
GROUP_DIM = 128
N_MIXERS = 2
NORM_EPS = 1e-6
SUBLN_EPS = 1e-5
NEG_INF = -1e30
LOG2_E = 1.4426950408889634

LANES = 128
VALUE_TILE_HEADS = 8
MXU_DIM = 256
VMEM_CAP = 56 * 1024 * 1024


def _lambda_init(layer):
    return 0.8 - 0.6 * math.exp(-0.3 * layer)


def _params(semantics, vmem_bytes):
    limit = int(min(VMEM_CAP, max(32 * 1024 * 1024, vmem_bytes * 5 // 4)))
    return pltpu.CompilerParams(dimension_semantics=semantics, vmem_limit_bytes=limit)


def _rms_rows(x, g, eps):
    ms = jnp.mean(x * x, axis=-1, keepdims=True)
    return x * lax.rsqrt(ms + eps) * g


def _group_rms(x, bsum, group, eps):
    x2 = x * x
    hi = x2.astype(BF16)
    lo = (x2 - hi.astype(F32)).astype(BF16)
    parts = []
    for c in range(x.shape[1] // MXU_DIM):
        sl = slice(c * MXU_DIM, (c + 1) * MXU_DIM)
        parts.append(jnp.dot(hi[:, sl], bsum, preferred_element_type=F32)
                     + jnp.dot(lo[:, sl], bsum, preferred_element_type=F32))
    ss = parts[0] if len(parts) == 1 else jnp.concatenate(parts, axis=1)
    return x * lax.rsqrt(ss * (1.0 / group) + eps)


def _group_sum_matrix(group):
    r = jnp.arange(MXU_DIM) // group
    return (r[:, None] == r[None, :]).astype(BF16)


def _diff_lambda_value(lam_ref, layer):
    lf = lam_ref[...]
    a = jnp.sum(lf[0:1] * lf[1:2], axis=-1, keepdims=True)
    b = jnp.sum(lf[2:3] * lf[3:4], axis=-1, keepdims=True)
    return jnp.exp(a) - jnp.exp(b) + _lambda_init(layer)


def _qkv_kernel(x_ref, g_ref, wq_ref, wk_ref, wv_ref, qg_ref, kg_ref, bsum_ref, *rest, feature_major):
    if feature_major:
        qt_ref, kb_ref, kt_ref, v_ref, vt_ref, h_scr = rest
    else:
        q_ref, k_ref, v_ref, h_scr = rest

    @pl.when(pl.program_id(1) == 0)
    def _():
        h_scr[...] = _rms_rows(x_ref[...], g_ref[...], NORM_EPS).astype(BF16)

    h = h_scr[...]
    bsum = bsum_ref[...]
    q_scale = QK_HEAD_DIM ** -0.5 * (LOG2_E if feature_major else 1.0)
    q = jnp.dot(h, wq_ref[...], preferred_element_type=F32)
    qn = _group_rms(q, bsum, QK_HEAD_DIM, NORM_EPS) * qg_ref[...] * q_scale
    k = jnp.dot(h, wk_ref[...], preferred_element_type=F32)
    kn = _group_rms(k, bsum, QK_HEAD_DIM, NORM_EPS) * kg_ref[...]
    v = jnp.dot(h, wv_ref[...], preferred_element_type=F32)
    v_ref[...] = v
    if feature_major:
        qt_ref[0] = qn.T.astype(BF16)
        kb_ref[...] = kn.astype(BF16)
        kt_ref[0] = kn.T
        vt_ref[0] = v.T.astype(BF16)
    else:
        q_ref[...] = qn
        k_ref[...] = kn


def _qkv(x, g, w_qkv, q_gain, k_gain, tm, tn, seq=None):
    m, d = x.shape
    n = w_qkv.shape[1] // 3
    nj = n // tn
    qg = jnp.tile(q_gain, tn // QK_HEAD_DIM)[None, :]
    kg = jnp.tile(k_gain, tn // QK_HEAD_DIM)[None, :]
    bsum = _group_sum_matrix(QK_HEAD_DIM)
    row = pl.BlockSpec((tm, d), lambda i, j: (i, 0))
    col = lambda off: pl.BlockSpec((d, tn), lambda i, j, off=off: (0, j + off))
    vec = lambda width: pl.BlockSpec((1, width), lambda i, j: (0, 0))
    out = pl.BlockSpec((tm, tn), lambda i, j: (i, j))
    if seq is None:
        out_specs = [out, out, out]
        out_shape = [jax.ShapeDtypeStruct((m, n), F32)] * 3
    else:
        per_seq = seq // tm
        tile_t = pl.BlockSpec((1, tn, tm), lambda i, j: (i, j, 0))
        seq_t = pl.BlockSpec((1, tn, tm), lambda i, j: (i // per_seq, j, i % per_seq))
        out_specs = [tile_t, out, seq_t, out, tile_t]
        out_shape = [jax.ShapeDtypeStruct((m // tm, n, tm), BF16), jax.ShapeDtypeStruct((m, n), BF16),
                     jax.ShapeDtypeStruct((m // seq, n, seq), F32), jax.ShapeDtypeStruct((m, n), F32),
                     jax.ShapeDtypeStruct((m // tm, n, tm), BF16)]
    vmem = 2 * tm * d * 4 + tm * d * 2 + 6 * d * tn * 2 + 2 * tm * tn * 16 + 10 * tm * tn * 4
    return pl.pallas_call(
        functools.partial(_qkv_kernel, feature_major=seq is not None),
        grid=(m // tm, nj),
        in_specs=[row, vec(d), col(0), col(nj), col(2 * nj), vec(tn), vec(tn),
                  pl.BlockSpec((MXU_DIM, MXU_DIM), lambda i, j: (0, 0))],
        out_specs=out_specs,
        out_shape=out_shape,
        scratch_shapes=[pltpu.VMEM((tm, d), BF16)],
        compiler_params=_params(("parallel", "arbitrary"), vmem),
        name="qkv",
    )(x, g[None, :], w_qkv, w_qkv, w_qkv, qg, kg, bsum)


def _mm_res_kernel(x_ref, a_ref, w_ref, o_ref):
    o_ref[...] = x_ref[...] + jnp.dot(a_ref[...].astype(BF16), w_ref[...],
                                      preferred_element_type=F32)


def _mm_res(x, a, w, tm, tn):
    m, n = x.shape
    kd = a.shape[1]
    vmem = 2 * tm * kd * a.dtype.itemsize + 2 * kd * tn * 2 + 4 * tm * tn * 4 + tm * kd * 2 + tm * tn * 4
    return pl.pallas_call(
        _mm_res_kernel,
        grid=(m // tm, n // tn),
        in_specs=[pl.BlockSpec((tm, tn), lambda i, j: (i, j)),
                  pl.BlockSpec((tm, kd), lambda i, j: (i, 0)),
                  pl.BlockSpec((kd, tn), lambda i, j: (0, j))],
        out_specs=pl.BlockSpec((tm, tn), lambda i, j: (i, j)),
        out_shape=jax.ShapeDtypeStruct((m, n), F32),
        compiler_params=_params(("parallel", "arbitrary"), vmem),
        name="mm_res",
    )(x, a, w)


def _ffn_kernel(x_ref, g_ref, w1_ref, w2_ref, o_ref, *rest, emit_weights):
    if emit_weights:
        w1b_ref, w2b_ref, h_scr = rest
    else:
        (h_scr,) = rest

    @pl.when(pl.program_id(1) == 0)
    def _():
        x = x_ref[...]
        h_scr[...] = _rms_rows(x, g_ref[...], NORM_EPS).astype(BF16)
        o_ref[...] = x

    w1 = w1_ref[...].astype(BF16)
    w2 = w2_ref[...].astype(BF16)
    if emit_weights:
        w1b_ref[...] = w1
        w2b_ref[...] = w2
    a = jnp.dot(h_scr[...], w1, preferred_element_type=F32)
    a = jnp.maximum(a, 0.0)
    a = (a * a).astype(BF16)
    o_ref[...] += jnp.dot(a, w2, preferred_element_type=F32)


def _ffn(x, g, w1, w2, tm, tf, layer=None):
    m, d = x.shape
    dff = w1.shape[-1]
    emit = layer is not None
    assert not emit or m == tm
    if emit:
        w1_spec = pl.BlockSpec((None, d, tf), lambda i, f: (layer, 0, f))
        w2_spec = pl.BlockSpec((None, tf, d), lambda i, f: (layer, f, 0))
    else:
        w1_spec = pl.BlockSpec((d, tf), lambda i, f: (0, f))
        w2_spec = pl.BlockSpec((tf, d), lambda i, f: (f, 0))
    out_specs = [pl.BlockSpec((tm, d), lambda i, f: (i, 0))]
    out_shape = [jax.ShapeDtypeStruct((m, d), F32)]
    if emit:
        out_specs += [pl.BlockSpec((d, tf), lambda i, f: (0, f)), pl.BlockSpec((tf, d), lambda i, f: (f, 0))]
        out_shape += [jax.ShapeDtypeStruct((d, dff), BF16), jax.ShapeDtypeStruct((dff, d), BF16)]
    vmem = (4 * tm * d * 4 + tm * d * 2 + 4 * d * tf * w1.dtype.itemsize + tm * tf * 8
            + (6 * d * tf * 2 if emit else 0))
    res = pl.pallas_call(
        functools.partial(_ffn_kernel, emit_weights=emit),
        grid=(m // tm, dff // tf),
        in_specs=[pl.BlockSpec((tm, d), lambda i, f: (i, 0)),
                  pl.BlockSpec((1, d), lambda i, f: (0, 0)),
                  w1_spec, w2_spec],
        out_specs=out_specs,
        out_shape=out_shape,
        scratch_shapes=[pltpu.VMEM((tm, d), BF16)],
        compiler_params=_params(("parallel", "arbitrary"), vmem),
        name="ffn",
    )(x, g[None, :], w1, w2)
    return res if emit else res[0]


def _gelu(x):
    return 0.5 * x * (1.0 + lax.erf(x * (2.0 ** -0.5)))


def _gmlp_kernel(x_ref, g_ref, wu_ref, wv_ref, vg_ref, bsum_ref, ws_ref, bs_ref, *rest, emit_cv):
    if emit_cv:
        gate_ref, cv_ref, h_scr = rest
    else:
        gate_ref, h_scr = rest

    @pl.when(pl.program_id(1) == 0)
    def _():
        h_scr[...] = _rms_rows(x_ref[...], g_ref[...], NORM_EPS).astype(BF16)

    h = h_scr[...]
    tm = h.shape[0]
    tn = wu_ref.shape[1]
    u = _gelu(jnp.dot(h, wu_ref[...], preferred_element_type=F32))
    v = _gelu(jnp.dot(h, wv_ref[...], preferred_element_type=F32))
    vn = _group_rms(v, bsum_ref[...], GROUP_DIM, NORM_EPS) * vg_ref[...]
    if emit_cv:
        cv_ref[...] = vn
    vb = vn.astype(BF16)
    t_idx = lax.broadcasted_iota(jnp.int32, (CHUNK, CHUNK), 0)
    s_idx = lax.broadcasted_iota(jnp.int32, (CHUNK, CHUNK), 1)
    causal = t_idx >= s_idx
    for gi in range(tn // GROUP_DIM):
        cols = slice(gi * GROUP_DIM, (gi + 1) * GROUP_DIM)
        w = jnp.where(causal, ws_ref[gi], 0.0).astype(BF16)
        bias = bs_ref[gi]
        for c in range(tm // CHUNK):
            rows = slice(c * CHUNK, (c + 1) * CHUNK)
            mixed = jnp.dot(w, vb[rows, cols], preferred_element_type=F32) + bias
            gate_ref[rows, cols] = (u[rows, cols] * mixed).astype(BF16)


def _gmlp_front(x, g, w_in, v_gain, ws, bs, tm, tn, emit_cv):
    m, d = x.shape
    dg = w_in.shape[1] // 2
    nj = dg // tn
    gpt = tn // GROUP_DIM
    blk = pl.BlockSpec((tm, tn), lambda i, j: (i, j))
    out_specs = [blk, blk] if emit_cv else [blk]
    out_shape = [jax.ShapeDtypeStruct((m, dg), BF16)]
    if emit_cv:
        out_shape.append(jax.ShapeDtypeStruct((m, dg), F32))
    vmem = 2 * tm * d * 4 + tm * d * 2 + 4 * d * tn * 2 + 4 * gpt * CHUNK * CHUNK * 4 + 12 * tm * tn * 4
    res = pl.pallas_call(
        functools.partial(_gmlp_kernel, emit_cv=emit_cv),
        grid=(m // tm, nj),
        in_specs=[pl.BlockSpec((tm, d), lambda i, j: (i, 0)),
                  pl.BlockSpec((1, d), lambda i, j: (0, 0)),
                  pl.BlockSpec((d, tn), lambda i, j: (0, j)),
                  pl.BlockSpec((d, tn), lambda i, j, nj=nj: (0, j + nj)),
                  pl.BlockSpec((1, tn), lambda i, j: (0, j)),
                  pl.BlockSpec((MXU_DIM, MXU_DIM), lambda i, j: (0, 0)),
                  pl.BlockSpec((gpt, CHUNK, CHUNK), lambda i, j: (j, 0, 0)),
                  pl.BlockSpec((gpt, CHUNK, GROUP_DIM), lambda i, j: (j, 0, 0))],
        out_specs=out_specs,
        out_shape=out_shape,
        scratch_shapes=[pltpu.VMEM((tm, d), BF16)],
        compiler_params=_params(("parallel", "arbitrary"), vmem),
        name="gmlp_front",
    )(x, g[None, :], w_in, w_in, v_gain[None, :], _group_sum_matrix(GROUP_DIM), ws, bs)
    return res if emit_cv else (res[0], None)


def _shifted_bias(dist, table_ref, head):
    max_exact = NUM_BUCKETS // 2
    n = jnp.maximum(dist, 0)
    nf = jnp.maximum(n, 1).astype(F32)
    large = max_exact + (jnp.log(nf / max_exact) / math.log(MAX_DISTANCE / max_exact)
                         * (NUM_BUCKETS - max_exact)).astype(jnp.int32)
    bucket = jnp.where(n < max_exact, n, jnp.minimum(large, NUM_BUCKETS - 1))
    far = table_ref[NUM_BUCKETS - 1, head]
    acc = jnp.zeros(dist.shape, F32)
    for b in range(NUM_BUCKETS - 1):
        acc = jnp.where(bucket == b, table_ref[b, head] - far, acc)
    return acc


def _prompt_bias_kernel(table_ref, o_ref):
    head = pl.program_id(0)
    key = lax.broadcasted_iota(jnp.int32, (LANES, LANES), 0)
    qry = lax.broadcasted_iota(jnp.int32, (LANES, LANES), 1)
    o_ref[0, 0] = jnp.where(qry >= key, _shifted_bias(qry - key, table_ref, head) * LOG2_E, NEG_INF)
    o_ref[0, 1] = _shifted_bias(LANES + qry - key, table_ref, head) * LOG2_E


def _prompt_bias(table, n_heads):
    return pl.pallas_call(
        _prompt_bias_kernel,
        grid=(n_heads,),
        in_specs=[pl.BlockSpec(memory_space=pltpu.SMEM)],
        out_specs=pl.BlockSpec((1, 2, LANES, LANES), lambda h: (h, 0, 0, 0)),
        out_shape=jax.ShapeDtypeStruct((n_heads, 2, LANES, LANES), F32),
        name="prompt_bias",
    )(table)


def _sample_bias_kernel(table_ref, last_ref, new_ref, *, n_heads, dec_seq, page):
    rows = 2 * n_heads * dec_seq
    ridx = lax.broadcasted_iota(jnp.int32, (rows, page), 0)
    key = lax.broadcasted_iota(jnp.int32, (rows, page), 1)
    tok = ridx % dec_seq
    row_head = (ridx % (n_heads * dec_seq)) // dec_seq
    last = jnp.zeros((rows, page), F32)
    new = jnp.zeros((rows, page), F32)
    for h in range(n_heads):
        last = jnp.where(row_head == h, _shifted_bias(page + tok - key, table_ref, h), last)
        new = jnp.where(row_head == h, _shifted_bias(tok - key, table_ref, h), new)
    last_ref[...] = last
    new_ref[...] = jnp.where(key <= tok, new, NEG_INF)


def _sample_bias(table, n_heads, dec_seq, page):
    rows = 2 * n_heads * dec_seq
    return pl.pallas_call(
        functools.partial(_sample_bias_kernel, n_heads=n_heads, dec_seq=dec_seq, page=page),
        in_specs=[pl.BlockSpec(memory_space=pltpu.SMEM)],
        out_specs=[pl.BlockSpec(memory_space=pltpu.VMEM), pl.BlockSpec(memory_space=pltpu.VMEM)],
        out_shape=[jax.ShapeDtypeStruct((rows, page), F32), jax.ShapeDtypeStruct((rows, page), F32)],
        name="sample_bias",
    )(table)


def _attn_prompt_kernel(qt_ref, k_ref, vt_ref, bias_ref, lam_ref, sg_ref, o_ref,
                        qs_scr, m_scr, l_scr, acc_scr, *, layer):
    i = pl.program_id(2)
    t = qt_ref.shape[2]
    hd = QK_HEAD_DIM
    dv = V_HEAD_DIM
    heads = qs_scr.shape[0]
    zero = jnp.zeros((hd, t), BF16)
    for h in range(heads):
        qs_scr[h, 0:hd, 0:t] = qt_ref[0, h * dv:h * dv + hd, :]
        qs_scr[h, hd:2 * hd, 0:t] = zero
        qs_scr[h, 0:hd, t:2 * t] = zero
        qs_scr[h, hd:2 * hd, t:2 * t] = qt_ref[0, h * dv + hd:(h + 1) * dv, :]
    m_scr[...] = jnp.full(m_scr.shape, NEG_INF, F32)
    l_scr[...] = jnp.zeros(l_scr.shape, F32)
    acc_scr[...] = jnp.zeros(acc_scr.shape, F32)

    far, prev, diag = 0, 1, 2
    qc = MXU_DIM
    per_map = t // qc
    chunks = [(h, c) for c in range(2 * per_map) for h in range(heads)]

    sub = LANES
    masked_tile = jnp.full((sub, sub), NEG_INF, F32)

    def with_tiles(s, tiles):
        out_rows = []
        for a in range(s.shape[0] // sub):
            row = s[a * sub:(a + 1) * sub]
            if any(ta == a for ta, _ in tiles):
                pieces = []
                for cb in range(qc // sub):
                    piece = row[:, cb * sub:(cb + 1) * sub]
                    if (a, cb) in tiles:
                        tile = tiles[(a, cb)]
                        piece = masked_tile if tile is None else piece + tile
                    pieces.append(piece)
                row = jnp.concatenate(pieces, axis=1)
            out_rows.append(row)
        return jnp.concatenate(out_rows, axis=0)

    def run(blocks):
        seq = [(bi, h, c) for bi in range(len(blocks)) for (h, c) in chunks]

        def n_keys(bi, c):
            return (c % per_map + 1) * qc if blocks[bi][1] == diag else t

        def scores(bi, h, c):
            j, kind = blocks[bi]
            nk, cq = n_keys(bi, c), c % per_map
            k = k_ref[pl.ds(pl.multiple_of(j * t, t), nk), h * dv:(h + 1) * dv]
            s = jnp.dot(k, qs_scr[h, :, c * qc:(c + 1) * qc], preferred_element_type=F32)
            first = cq * (qc // sub)
            if kind == diag:
                tiles = {}
                for cb in range(qc // sub):
                    tiles[(first + cb, cb)] = bias_ref[h, 0]
                    if first + cb >= 1:
                        tiles[(first + cb - 1, cb)] = bias_ref[h, 1]
                    for a in range(first + cb + 1, nk // sub):
                        tiles[(a, cb)] = None
                s = with_tiles(s, tiles)
            elif kind == prev and cq == 0:
                s = with_tiles(s, {(t // sub - 1, 0): bias_ref[h, 1]})
            return s

        def softmax_update(h, c, s):
            cols = slice(c * qc, (c + 1) * qc)
            m_prev = m_scr[h, :, cols]
            m_new = jnp.maximum(m_prev, jnp.max(s, axis=0, keepdims=True))
            alpha = jnp.exp2(m_prev - m_new)
            p = jnp.exp2(s - m_new)
            l_scr[h, :, cols] = alpha * l_scr[h, :, cols] + jnp.sum(p, axis=0, keepdims=True)
            m_scr[h, :, cols] = m_new
            return p.astype(BF16), alpha

        def weigh(bi, h, c, p, alpha):
            cols = slice(c * qc, (c + 1) * qc)
            acc_scr[h, :, cols] = alpha * acc_scr[h, :, cols] + jnp.dot(
                vt_ref[blocks[bi][0], h * dv:(h + 1) * dv, 0:n_keys(bi, c)], p,
                preferred_element_type=F32)

        ahead = ATTN_SCORES_AHEAD
        s = {n: scores(*seq[n]) for n in range(min(ahead, len(seq)))}
        for n, (bi, h, c) in enumerate(seq):
            p, alpha = softmax_update(h, c, s.pop(n))
            if n + ahead < len(seq):
                s[n + ahead] = scores(*seq[n + ahead])
            weigh(bi, h, c, p, alpha)

    n_far = i - 1

    def far_pair(jj, carry):
        run([(2 * jj, far), (2 * jj + 1, far)])
        return carry

    lax.fori_loop(0, lax.shift_right_arithmetic(n_far, 1), far_pair, 0)

    @pl.when((n_far >= 1) & ((n_far & 1) == 1))
    def _():
        run([(n_far - 1, far)])

    @pl.when(i >= 1)
    def _():
        run([(i - 1, prev), (i, diag)])

    @pl.when(i == 0)
    def _():
        run([(i, diag)])

    lam = _diff_lambda_value(lam_ref, layer)
    for h in range(heads):
        o = acc_scr[h] / l_scr[h]
        od = o[:, 0:t] - lam * o[:, t:2 * t]
        on = od * lax.rsqrt(jnp.mean(od * od, axis=0, keepdims=True) + SUBLN_EPS)
        o_ref[:, h * dv:(h + 1) * dv] = (on.T * sg_ref[...] * (1.0 - _lambda_init(layer))).astype(BF16)


def _attn_prompt(qt, kb, vt, bias, lam, subln, batch, layer):
    n_tiles, d, tile = qt.shape
    m = kb.shape[0]
    seq = m // batch
    nq = seq // tile
    heads = ATTN_HEADS_PER_STEP
    width = heads * V_HEAD_DIM
    assert d % width == 0
    vmem = 4 * seq * width * 2 + 16 * tile * MXU_DIM * 4 + 4 * width * 2 * tile * 4
    return pl.pallas_call(
        functools.partial(_attn_prompt_kernel, layer=layer),
        grid=(batch, d // width, nq),
        in_specs=[pl.BlockSpec((1, width, tile), lambda b, g, i: (b * nq + i, g, 0)),
                  pl.BlockSpec((seq, width), lambda b, g, i: (b, g)),
                  pl.BlockSpec((nq, width, tile), lambda b, g, i: (b, g, 0)),
                  pl.BlockSpec((heads, 2, LANES, LANES), lambda b, g, i: (g, 0, 0, 0)),
                  pl.BlockSpec(lam.shape, lambda b, g, i: (0, 0)),
                  pl.BlockSpec((1, V_HEAD_DIM), lambda b, g, i: (0, 0))],
        out_specs=pl.BlockSpec((tile, width), lambda b, g, i: (b * nq + i, g)),
        out_shape=jax.ShapeDtypeStruct((m, d), BF16),
        scratch_shapes=[pltpu.VMEM((heads, V_HEAD_DIM, 2 * tile), BF16),
                        pltpu.VMEM((heads, 1, 2 * tile), F32),
                        pltpu.VMEM((heads, 1, 2 * tile), F32),
                        pltpu.VMEM((heads, V_HEAD_DIM, 2 * tile), F32)],
        compiler_params=_params(("parallel", "parallel", "arbitrary"), vmem),
        name="attn_prompt",
    )(qt, kb, vt, bias, lam, subln[None, :])


def _attn_sample_kernel(pt_ref, q_ref, kn_ref, vn_ref, kc_hbm, vc_hbm, blast_ref, bnew_ref, lam_ref,
                        sg_ref, o_ref, qbd_scr, lg_scr, w_scr, wn_scr, acc_scr, kbuf, vbuf, ksem, vsem,
                        *, layer, n_pages, dec_seq, slots):
    b = pl.program_id(0)
    s = pl.program_id(1)
    d = q_ref.shape[2]
    n_heads = d // V_HEAD_DIM
    v_parts = n_heads // VALUE_TILE_HEADS
    k_steps = n_pages // slots
    rows = 2 * n_heads * dec_seq
    half = rows // 2
    page = kbuf.shape[3]
    group_rows = acc_scr.shape[1]
    group_heads = group_rows // dec_seq

    def key_copies(bb, grp):
        return [pltpu.make_async_copy(kc_hbm.at[pt_ref[bb, grp * slots + e]], kbuf.at[grp, e], ksem.at[grp])
                for e in range(slots)]

    def value_copies(bb, grp):
        return [pltpu.make_async_copy(
                    vc_hbm.at[pt_ref[bb, grp * slots + e], :, pl.ds(part * VALUE_TILE_HEADS, VALUE_TILE_HEADS), :],
                    vbuf.at[grp, e, part], vsem.at[grp])
                for e in range(slots) for part in range(v_parts)]

    @pl.when((b == 0) & (s == 0))
    def _():
        for grp in range(k_steps):
            for cp in key_copies(0, grp):
                cp.start()

    @pl.when(s < k_steps)
    def _():
        for cp in value_copies(b, s):
            cp.start()

    @pl.when((s >= k_steps) & (b + 1 < pl.num_programs(0)))
    def _():
        for cp in key_copies(b + 1, s - k_steps):
            cp.start()

    def padded_new(ref):
        x = ref[0]
        return jnp.concatenate([x, jnp.zeros((page - x.shape[0], d), F32)], axis=0).astype(BF16)

    @pl.when(s == 0)
    def _():
        q = q_ref[0]
        qrep = jnp.broadcast_to(q[None], (rows // q.shape[0], q.shape[0], d)).reshape(rows, d)
        ridx = lax.broadcasted_iota(jnp.int32, (rows, d), 0)
        lane = lax.broadcasted_iota(jnp.int32, (rows, d), 1)
        row_head = (ridx % half) // dec_seq
        row_map = ridx // half
        match = (lane // V_HEAD_DIM == row_head) & ((lane // QK_HEAD_DIM) % 2 == row_map)
        qbd_scr[...] = jnp.where(match, qrep, 0.0).astype(BF16)
        acc_scr[...] = jnp.zeros(acc_scr.shape, F32)

    @pl.when(s < k_steps)
    def _():
        for cp in key_copies(b, s):
            cp.wait()
        for e in range(slots):
            lg_scr[s * slots + e] = jnp.dot(qbd_scr[...], kbuf[s, e].astype(BF16),
                                            preferred_element_type=F32)

    @pl.when(s == k_steps - 1)
    def _():
        lam = _diff_lambda_value(lam_ref, layer)
        lg_scr[n_pages - 1] = lg_scr[n_pages - 1] + blast_ref[...]
        lgn = lax.dot_general(qbd_scr[...], padded_new(kn_ref), (((1,), (1,)), ((), ())),
                              preferred_element_type=F32) + bnew_ref[...]
        lg = lg_scr[...]
        m = jnp.maximum(jnp.max(jnp.max(lg, axis=0), axis=-1, keepdims=True),
                        jnp.max(lgn, axis=-1, keepdims=True))
        p = jnp.exp(lg - m[None])
        pn = jnp.exp(lgn - m)
        l = jnp.sum(jnp.sum(p, axis=0), axis=-1, keepdims=True) + jnp.sum(pn, axis=-1, keepdims=True)
        inv = 1.0 / l
        p = p * inv[None]
        pn = pn * inv
        w_scr[...] = p[:, 0:half, :] - lam * p[:, half:rows, :]
        wn_scr[...] = (pn[0:half, :] - lam * pn[half:rows, :]).astype(BF16)

    @pl.when(s >= k_steps)
    def _():
        grp = s - k_steps
        for cp in value_copies(b, grp):
            cp.wait()
        for g in range(n_heads // group_heads):
            part, first = divmod(g * group_heads, VALUE_TILE_HEADS)
            acc = acc_scr[g]
            for e in range(slots):
                vflat = vbuf.at[grp, e, part].reshape(page * VALUE_TILE_HEADS, V_HEAD_DIM)
                v = jnp.concatenate(
                    [vflat[pl.ds(first + hh, page, stride=VALUE_TILE_HEADS), :].astype(BF16)
                     for hh in range(group_heads)], axis=1)
                w = w_scr[grp * slots + e, g * group_rows:(g + 1) * group_rows, :]
                acc = acc + jnp.dot(w.astype(BF16), v, preferred_element_type=F32)
            acc_scr[g] = acc

    @pl.when(s == 2 * k_steps - 1)
    def _():
        new = jnp.dot(wn_scr[...], padded_new(vn_ref), preferred_element_type=F32)
        scale = 1.0 - _lambda_init(layer)
        for h in range(n_heads):
            g, e = divmod(h, group_heads)
            cols = slice(h * V_HEAD_DIM, (h + 1) * V_HEAD_DIM)
            piece = (acc_scr[g, e * dec_seq:(e + 1) * dec_seq, e * V_HEAD_DIM:(e + 1) * V_HEAD_DIM]
                     + new[h * dec_seq:(h + 1) * dec_seq, cols])
            o_ref[0, :, cols] = _rms_rows(piece, sg_ref[...], SUBLN_EPS) * scale


def _attn_sample(q, k_new, v_new, cache_k, cache_v, page_table, blast, bnew, lam, subln, layer):
    batch, dec_seq, d = q.shape
    n_pages = page_table.shape[1]
    page = cache_k.shape[2]
    n_heads = d // V_HEAD_DIM
    rows = 2 * n_heads * dec_seq
    sub_f32, sub_bf16 = 8, 16
    assert sub_f32 % dec_seq == 0
    q = jnp.tile(q, (1, sub_f32 // dec_seq, 1))
    k_new = jnp.pad(k_new, ((0, 0), (0, sub_bf16 - dec_seq), (0, 0)))
    v_new = jnp.pad(v_new, ((0, 0), (0, sub_bf16 - dec_seq), (0, 0)))
    tok = lambda n: pl.BlockSpec((1, n, d), lambda b, s, pt: (b, 0, 0))
    full = lambda a: pl.BlockSpec(a.shape, lambda b, s, pt: (0,) * a.ndim)
    slots = SAMPLE_PAGE_SLOTS
    assert n_pages % slots == 0
    k_steps = n_pages // slots
    v_parts = n_heads // VALUE_TILE_HEADS
    grid_spec = pltpu.PrefetchScalarGridSpec(
        num_scalar_prefetch=1,
        grid=(batch, 2 * k_steps),
        in_specs=[tok(sub_f32), tok(sub_bf16), tok(sub_bf16),
                  pl.BlockSpec(memory_space=pl.ANY), pl.BlockSpec(memory_space=pl.ANY),
                  full(blast), full(bnew), full(lam),
                  pl.BlockSpec((1, V_HEAD_DIM), lambda b, s, pt: (0, 0))],
        out_specs=pl.BlockSpec((1, dec_seq, d), lambda b, s, pt: (b, 0, 0)),
        scratch_shapes=[pltpu.VMEM((rows, d), BF16),
                        pltpu.VMEM((n_pages, rows, page), F32),
                        pltpu.VMEM((n_pages, rows // 2, page), F32),
                        pltpu.VMEM((rows // 2, page), BF16),
                        pltpu.VMEM((n_heads * dec_seq // sub_f32, sub_f32,
                                    sub_f32 // dec_seq * V_HEAD_DIM), F32),
                        pltpu.VMEM((k_steps, slots, d, page), F32),
                        pltpu.VMEM((k_steps, slots, v_parts, page, VALUE_TILE_HEADS, V_HEAD_DIM), F32),
                        pltpu.SemaphoreType.DMA((k_steps,)),
                        pltpu.SemaphoreType.DMA((k_steps,))],
    )
    vmem = 2 * k_steps * slots * page * d * 4 + 6 * n_pages * rows * page * 4 + 4 * page * d * 2
    return pl.pallas_call(
        functools.partial(_attn_sample_kernel, layer=layer, n_pages=n_pages, dec_seq=dec_seq, slots=slots),
        grid_spec=grid_spec,
        out_shape=jax.ShapeDtypeStruct((batch, dec_seq, d), F32),
        compiler_params=_params(("arbitrary", "arbitrary"), vmem),
        name="attn_sample",
    )(page_table, q, k_new, v_new, cache_k, cache_v, blast, bnew, lam, subln[None, :])


PROMPT_TM = 512
ATTN_TILE = 512
PROJ_TN = 512
RES_TN = 2048
FFN_TF = 1024
FFN_CAST_TF = 512
SAMPLE_PAGE_SLOTS = 8
ATTN_HEADS_PER_STEP = 4
ATTN_SCORES_AHEAD = 4


def kernel(x_prompt, x_sample, cache_k, cache_v, page_table, rel_bias_table, mix_norm, ffn_norm,
           attn_w_qkv, attn_w_o, attn_q_norm, attn_k_norm, attn_lambda, attn_subln,
           cm_w_in, cm_v_norm, cm_w_s, cm_b_s, cm_w_o, ffn_w1, ffn_w2):
    batch, seq, d = x_prompt.shape
    dec_batch, dec_seq, _ = x_sample.shape
    depth = mix_norm.shape[0]
    n_heads = d // V_HEAD_DIM
    n_groups = cm_w_s.shape[1]
    page = cache_k.shape[2]
    mp, ms = batch * seq, dec_batch * dec_seq
    assert seq % ATTN_TILE == 0 and mp % PROMPT_TM == 0 and ms % CHUNK == 0 and CHUNK % dec_seq == 0

    xp = x_prompt.reshape(mp, d)
    xs = x_sample.reshape(ms, d)
    ck = jnp.transpose(cache_k, (0, 1, 3, 4, 5, 2)).reshape(cache_k.shape[0], cache_k.shape[1], d, page)
    cv = cache_v

    bias_p = _prompt_bias(rel_bias_table, n_heads)
    blast, bnew = _sample_bias(rel_bias_table, n_heads, dec_seq, page)

    k_prompt, v_prompt, k_sample, v_sample, cv_sample = [], [], [], [], []
    for i in range(depth):
        j = i // N_MIXERS
        if i % N_MIXERS == 0:
            w_qkv = attn_w_qkv[j].astype(BF16)
            w_o = attn_w_o[j].astype(BF16)
            qtp, kbp, ktp, vp, vtp = _qkv(xp, mix_norm[i], w_qkv, attn_q_norm[j], attn_k_norm[j],
                                          ATTN_TILE, PROJ_TN, seq=seq)
            qs, ks, vs = _qkv(xs, mix_norm[i], w_qkv, attn_q_norm[j], attn_k_norm[j], ms, PROJ_TN)
            op = _attn_prompt(qtp, kbp, vtp, bias_p, attn_lambda[j], attn_subln[j], batch, i)
            os_ = _attn_sample(qs.reshape(dec_batch, dec_seq, d),
                               ks.reshape(dec_batch, dec_seq, d), vs.reshape(dec_batch, dec_seq, d),
                               ck[j], cv[j], page_table, blast, bnew, attn_lambda[j], attn_subln[j], i)
            xp = _mm_res(xp, op, w_o, PROMPT_TM, RES_TN)
            xs = _mm_res(xs, os_.reshape(ms, d), w_o, ms, RES_TN)
            k_prompt.append(jnp.transpose(ktp.reshape(batch, n_heads, 2, QK_HEAD_DIM, seq), (0, 4, 1, 2, 3)))
            v_prompt.append(vp.reshape(batch, seq, n_heads, V_HEAD_DIM))
            k_sample.append(ks.reshape(dec_batch, dec_seq, n_heads, 2, QK_HEAD_DIM))
            v_sample.append(vs.reshape(dec_batch, dec_seq, n_heads, V_HEAD_DIM))
        else:
            w_in = cm_w_in[j].astype(BF16)
            w_o = cm_w_o[j].astype(BF16)
            bs_p = jnp.broadcast_to(cm_b_s[j][:, :, None], (n_groups, CHUNK, GROUP_DIM))
            gp, _ = _gmlp_front(xp, mix_norm[i], w_in, cm_v_norm[j], cm_w_s[j], bs_p,
                                PROMPT_TM, PROJ_TN, False)
            reps = CHUNK // dec_seq
            eye = jnp.eye(reps, dtype=F32)
            ws_s = (eye[None, :, None, :, None] * cm_w_s[j][:, None, :dec_seq, None, :dec_seq]
                    ).reshape(n_groups, CHUNK, CHUNK)
            bs_s = jnp.broadcast_to(jnp.tile(cm_b_s[j][:, :dec_seq], (1, reps))[:, :, None],
                                    (n_groups, CHUNK, GROUP_DIM))
            gs, cvs = _gmlp_front(xs, mix_norm[i], w_in, cm_v_norm[j], ws_s, bs_s, ms, PROJ_TN, True)
            xp = _mm_res(xp, gp, w_o, PROMPT_TM, RES_TN)
            xs = _mm_res(xs, gs, w_o, ms, RES_TN)
            cv_sample.append(cvs.reshape(dec_batch, dec_seq, n_groups, GROUP_DIM))
        xs, w1, w2 = _ffn(xs, ffn_norm[i], ffn_w1, ffn_w2, ms, FFN_CAST_TF, layer=i)
        xp = _ffn(xp, ffn_norm[i], w1, w2, PROMPT_TM, FFN_TF)

    return (xp.reshape(batch, seq, d), xs.reshape(dec_batch, dec_seq, d),
            jnp.stack(k_prompt), jnp.stack(v_prompt), jnp.stack(k_sample), jnp.stack(v_sample),
            jnp.stack(cv_sample))
```

```python
import functools
import math

import jax
import jax.numpy as jnp
from jax import lax
from jax.experimental import pallas as pl
from jax.experimental.pallas import tpu as pltpu

F32 = jnp.float32
BF16 = jnp.bfloat16

QK_HEAD_DIM = 64
V_HEAD_DIM = 128
NUM_BUCKETS = 32
MAX_DISTANCE = 128
CHUNK = 128
GROUP_DIM = 128
N_MIXERS = 2
NORM_EPS = 1e-6
SUBLN_EPS = 1e-5
NEG_INF = -1e30
LOG2_E = 1.4426950408889634

LANES = 128
VALUE_TILE_HEADS = 8
MXU_DIM = 256
VMEM_CAP = 56 * 1024 * 1024


def _lambda_init(layer):
    return 0.8 - 0.6 * math.exp(-0.3 * layer)


def _params(semantics, vmem_bytes):
    limit = int(min(VMEM_CAP, max(32 * 1024 * 1024, vmem_bytes * 5 // 4)))
    return pltpu.CompilerParams(dimension_semantics=semantics, vmem_limit_bytes=limit)


def _rms_rows(x, g, eps):
    ms = jnp.mean(x * x, axis=-1, keepdims=True)
    return x * lax.rsqrt(ms + eps) * g


def _group_rms(x, bsum, group, eps):
    x2 = x * x
    hi = x2.astype(BF16)
    lo = (x2 - hi.astype(F32)).astype(BF16)
    parts = []
    for c in range(x.shape[1] // MXU_DIM):
        sl = slice(c * MXU_DIM, (c + 1) * MXU_DIM)
        parts.append(jnp.dot(hi[:, sl], bsum, preferred_element_type=F32)
                     + jnp.dot(lo[:, sl], bsum, preferred_element_type=F32))
    ss = parts[0] if len(parts) == 1 else jnp.concatenate(parts, axis=1)
    return x * lax.rsqrt(ss * (1.0 / group) + eps)


def _group_sum_matrix(group):
    r = jnp.arange(MXU_DIM) // group
    return (r[:, None] == r[None, :]).astype(BF16)


def _diff_lambda_value(lam_ref, layer):
    lf = lam_ref[...]
    a = jnp.sum(lf[0:1] * lf[1:2], axis=-1, keepdims=True)
    b = jnp.sum(lf[2:3] * lf[3:4], axis=-1, keepdims=True)
    return jnp.exp(a) - jnp.exp(b) + _lambda_init(layer)


def _qkv_kernel(x_ref, g_ref, wq_ref, wk_ref, wv_ref, qg_ref, kg_ref, bsum_ref, *rest, feature_major):
    if feature_major:
        qt_ref, kb_ref, kt_ref, v_ref, vt_ref, h_scr = rest
    else:
        q_ref, k_ref, v_ref, h_scr = rest

    @pl.when(pl.program_id(1) == 0)
    def _():
        h_scr[...] = _rms_rows(x_ref[...], g_ref[...], NORM_EPS).astype(BF16)

    h = h_scr[...]
    bsum = bsum_ref[...]
    q_scale = QK_HEAD_DIM ** -0.5 * (LOG2_E if feature_major else 1.0)
    q = jnp.dot(h, wq_ref[...], preferred_element_type=F32)
    qn = _group_rms(q, bsum, QK_HEAD_DIM, NORM_EPS) * qg_ref[...] * q_scale
    k = jnp.dot(h, wk_ref[...], preferred_element_type=F32)
    kn = _group_rms(k, bsum, QK_HEAD_DIM, NORM_EPS) * kg_ref[...]
    v = jnp.dot(h, wv_ref[...], preferred_element_type=F32)
    v_ref[...] = v
    if feature_major:
        qt_ref[0] = qn.T.astype(BF16)
        kb_ref[...] = kn.astype(BF16)
        kt_ref[0] = kn.T
        vt_ref[0] = v.T.astype(BF16)
    else:
        q_ref[...] = qn
        k_ref[...] = kn


def _qkv(x, g, w_qkv, q_gain, k_gain, tm, tn, seq=None):
    m, d = x.shape
    n = w_qkv.shape[1] // 3
    nj = n // tn
    qg = jnp.tile(q_gain, tn // QK_HEAD_DIM)[None, :]
    kg = jnp.tile(k_gain, tn // QK_HEAD_DIM)[None, :]
    bsum = _group_sum_matrix(QK_HEAD_DIM)
    row = pl.BlockSpec((tm, d), lambda i, j: (i, 0))
    col = lambda off: pl.BlockSpec((d, tn), lambda i, j, off=off: (0, j + off))
    vec = lambda width: pl.BlockSpec((1, width), lambda i, j: (0, 0))
    out = pl.BlockSpec((tm, tn), lambda i, j: (i, j))
    if seq is None:
        out_specs = [out, out, out]
        out_shape = [jax.ShapeDtypeStruct((m, n), F32)] * 3
    else:
        per_seq = seq // tm
        tile_t = pl.BlockSpec((1, tn, tm), lambda i, j: (i, j, 0))
        seq_t = pl.BlockSpec((1, tn, tm), lambda i, j: (i // per_seq, j, i % per_seq))
        out_specs = [tile_t, out, seq_t, out, tile_t]
        out_shape = [jax.ShapeDtypeStruct((m // tm, n, tm), BF16), jax.ShapeDtypeStruct((m, n), BF16),
                     jax.ShapeDtypeStruct((m // seq, n, seq), F32), jax.ShapeDtypeStruct((m, n), F32),
                     jax.ShapeDtypeStruct((m // tm, n, tm), BF16)]
    vmem = 2 * tm * d * 4 + tm * d * 2 + 6 * d * tn * 2 + 2 * tm * tn * 16 + 10 * tm * tn * 4
    return pl.pallas_call(
        functools.partial(_qkv_kernel, feature_major=seq is not None),
        grid=(m // tm, nj),
        in_specs=[row, vec(d), col(0), col(nj), col(2 * nj), vec(tn), vec(tn),
                  pl.BlockSpec((MXU_DIM, MXU_DIM), lambda i, j: (0, 0))],
        out_specs=out_specs,
        out_shape=out_shape,
        scratch_shapes=[pltpu.VMEM((tm, d), BF16)],
        compiler_params=_params(("parallel", "arbitrary"), vmem),
        name="qkv",
    )(x, g[None, :], w_qkv, w_qkv, w_qkv, qg, kg, bsum)


def _mm_res_kernel(x_ref, a_ref, w_ref, o_ref):
    o_ref[...] = x_ref[...] + jnp.dot(a_ref[...].astype(BF16), w_ref[...],
                                      preferred_element_type=F32)


def _mm_res(x, a, w, tm, tn):
    m, n = x.shape
    kd = a.shape[1]
    vmem = 2 * tm * kd * a.dtype.itemsize + 2 * kd * tn * 2 + 4 * tm * tn * 4 + tm * kd * 2 + tm * tn * 4
    return pl.pallas_call(
        _mm_res_kernel,
        grid=(m // tm, n // tn),
        in_specs=[pl.BlockSpec((tm, tn), lambda i, j: (i, j)),
                  pl.BlockSpec((tm, kd), lambda i, j: (i, 0)),
                  pl.BlockSpec((kd, tn), lambda i, j: (0, j))],
        out_specs=pl.BlockSpec((tm, tn), lambda i, j: (i, j)),
        out_shape=jax.ShapeDtypeStruct((m, n), F32),
        compiler_params=_params(("parallel", "arbitrary"), vmem),
        name="mm_res",
    )(x, a, w)


def _ffn_kernel(x_ref, g_ref, w1_ref, w2_ref, o_ref, *rest, emit_weights):
    if emit_weights:
        w1b_ref, w2b_ref, h_scr = rest
    else:
        (h_scr,) = rest

    @pl.when(pl.program_id(1) == 0)
    def _():
        x = x_ref[...]
        h_scr[...] = _rms_rows(x, g_ref[...], NORM_EPS).astype(BF16)
        o_ref[...] = x

    w1 = w1_ref[...].astype(BF16)
    w2 = w2_ref[...].astype(BF16)
    if emit_weights:
        w1b_ref[...] = w1
        w2b_ref[...] = w2
    a = jnp.dot(h_scr[...], w1, preferred_element_type=F32)
    a = jnp.maximum(a, 0.0)
    a = (a * a).astype(BF16)
    o_ref[...] += jnp.dot(a, w2, preferred_element_type=F32)


def _ffn(x, g, w1, w2, tm, tf, layer=None):
    m, d = x.shape
    dff = w1.shape[-1]
    emit = layer is not None
    assert not emit or m == tm
    if emit:
        w1_spec = pl.BlockSpec((None, d, tf), lambda i, f: (layer, 0, f))
        w2_spec = pl.BlockSpec((None, tf, d), lambda i, f: (layer, f, 0))
    else:
        w1_spec = pl.BlockSpec((d, tf), lambda i, f: (0, f))
        w2_spec = pl.BlockSpec((tf, d), lambda i, f: (f, 0))
    out_specs = [pl.BlockSpec((tm, d), lambda i, f: (i, 0))]
    out_shape = [jax.ShapeDtypeStruct((m, d), F32)]
    if emit:
        out_specs += [pl.BlockSpec((d, tf), lambda i, f: (0, f)), pl.BlockSpec((tf, d), lambda i, f: (f, 0))]
        out_shape += [jax.ShapeDtypeStruct((d, dff), BF16), jax.ShapeDtypeStruct((dff, d), BF16)]
    vmem = (4 * tm * d * 4 + tm * d * 2 + 4 * d * tf * w1.dtype.itemsize + tm * tf * 8
            + (6 * d * tf * 2 if emit else 0))
    res = pl.pallas_call(
        functools.partial(_ffn_kernel, emit_weights=emit),
        grid=(m // tm, dff // tf),
        in_specs=[pl.BlockSpec((tm, d), lambda i, f: (i, 0)),
                  pl.BlockSpec((1, d), lambda i, f: (0, 0)),
                  w1_spec, w2_spec],
        out_specs=out_specs,
        out_shape=out_shape,
        scratch_shapes=[pltpu.VMEM((tm, d), BF16)],
        compiler_params=_params(("parallel", "arbitrary"), vmem),
        name="ffn",
    )(x, g[None, :], w1, w2)
    return res if emit else res[0]


def _gelu(x):
    return 0.5 * x * (1.0 + lax.erf(x * (2.0 ** -0.5)))


def _gmlp_kernel(x_ref, g_ref, wu_ref, wv_ref, vg_ref, bsum_ref, ws_ref, bs_ref, *rest, emit_cv):
    if emit_cv:
        gate_ref, cv_ref, h_scr = rest
    else:
        gate_ref, h_scr = rest

    @pl.when(pl.program_id(1) == 0)
    def _():
        h_scr[...] = _rms_rows(x_ref[...], g_ref[...], NORM_EPS).astype(BF16)

    h = h_scr[...]
    tm = h.shape[0]
    tn = wu_ref.shape[1]
    u = _gelu(jnp.dot(h, wu_ref[...], preferred_element_type=F32))
    v = _gelu(jnp.dot(h, wv_ref[...], preferred_element_type=F32))
    vn = _group_rms(v, bsum_ref[...], GROUP_DIM, NORM_EPS) * vg_ref[...]
    if emit_cv:
        cv_ref[...] = vn
    vb = vn.astype(BF16)
    t_idx = lax.broadcasted_iota(jnp.int32, (CHUNK, CHUNK), 0)
    s_idx = lax.broadcasted_iota(jnp.int32, (CHUNK, CHUNK), 1)
    causal = t_idx >= s_idx
    for gi in range(tn // GROUP_DIM):
        cols = slice(gi * GROUP_DIM, (gi + 1) * GROUP_DIM)
        w = jnp.where(causal, ws_ref[gi], 0.0).astype(BF16)
        bias = bs_ref[gi]
        for c in range(tm // CHUNK):
            rows = slice(c * CHUNK, (c + 1) * CHUNK)
            mixed = jnp.dot(w, vb[rows, cols], preferred_element_type=F32) + bias
            gate_ref[rows, cols] = (u[rows, cols] * mixed).astype(BF16)


def _gmlp_front(x, g, w_in, v_gain, ws, bs, tm, tn, emit_cv):
    m, d = x.shape
    dg = w_in.shape[1] // 2
    nj = dg // tn
    gpt = tn // GROUP_DIM
    blk = pl.BlockSpec((tm, tn), lambda i, j: (i, j))
    out_specs = [blk, blk] if emit_cv else [blk]
    out_shape = [jax.ShapeDtypeStruct((m, dg), BF16)]
    if emit_cv:
        out_shape.append(jax.ShapeDtypeStruct((m, dg), F32))
    vmem = 2 * tm * d * 4 + tm * d * 2 + 4 * d * tn * 2 + 4 * gpt * CHUNK * CHUNK * 4 + 12 * tm * tn * 4
    res = pl.pallas_call(
        functools.partial(_gmlp_kernel, emit_cv=emit_cv),
        grid=(m // tm, nj),
        in_specs=[pl.BlockSpec((tm, d), lambda i, j: (i, 0)),
                  pl.BlockSpec((1, d), lambda i, j: (0, 0)),
                  pl.BlockSpec((d, tn), lambda i, j: (0, j)),
                  pl.BlockSpec((d, tn), lambda i, j, nj=nj: (0, j + nj)),
                  pl.BlockSpec((1, tn), lambda i, j: (0, j)),
                  pl.BlockSpec((MXU_DIM, MXU_DIM), lambda i, j: (0, 0)),
                  pl.BlockSpec((gpt, CHUNK, CHUNK), lambda i, j: (j, 0, 0)),
                  pl.BlockSpec((gpt, CHUNK, GROUP_DIM), lambda i, j: (j, 0, 0))],
        out_specs=out_specs,
        out_shape=out_shape,
        scratch_shapes=[pltpu.VMEM((tm, d), BF16)],
        compiler_params=_params(("parallel", "arbitrary"), vmem),
        name="gmlp_front",
    )(x, g[None, :], w_in, w_in, v_gain[None, :], _group_sum_matrix(GROUP_DIM), ws, bs)
    return res if emit_cv else (res[0], None)


def _shifted_bias(dist, table_ref, head):
    max_exact = NUM_BUCKETS // 2
    n = jnp.maximum(dist, 0)
    nf = jnp.maximum(n, 1).astype(F32)
    large = max_exact + (jnp.log(nf / max_exact) / math.log(MAX_DISTANCE / max_exact)
                         * (NUM_BUCKETS - max_exact)).astype(jnp.int32)
    bucket = jnp.where(n < max_exact, n, jnp.minimum(large, NUM_BUCKETS - 1))
    far = table_ref[NUM_BUCKETS - 1, head]
    acc = jnp.zeros(dist.shape, F32)
    for b in range(NUM_BUCKETS - 1):
        acc = jnp.where(bucket == b, table_ref[b, head] - far, acc)
    return acc


def _prompt_bias_kernel(table_ref, o_ref):
    head = pl.program_id(0)
    key = lax.broadcasted_iota(jnp.int32, (LANES, LANES), 0)
    qry = lax.broadcasted_iota(jnp.int32, (LANES, LANES), 1)
    o_ref[0, 0] = jnp.where(qry >= key, _shifted_bias(qry - key, table_ref, head) * LOG2_E, NEG_INF)
    o_ref[0, 1] = _shifted_bias(LANES + qry - key, table_ref, head) * LOG2_E


def _prompt_bias(table, n_heads):
    return pl.pallas_call(
        _prompt_bias_kernel,
        grid=(n_heads,),
        in_specs=[pl.BlockSpec(memory_space=pltpu.SMEM)],
        out_specs=pl.BlockSpec((1, 2, LANES, LANES), lambda h: (h, 0, 0, 0)),
        out_shape=jax.ShapeDtypeStruct((n_heads, 2, LANES, LANES), F32),
        name="prompt_bias",
    )(table)


def _sample_bias_kernel(table_ref, last_ref, new_ref, *, n_heads, dec_seq, page):
    rows = 2 * n_heads * dec_seq
    ridx = lax.broadcasted_iota(jnp.int32, (rows, page), 0)
    key = lax.broadcasted_iota(jnp.int32, (rows, page), 1)
    tok = ridx % dec_seq
    row_head = (ridx % (n_heads * dec_seq)) // dec_seq
    last = jnp.zeros((rows, page), F32)
    new = jnp.zeros((rows, page), F32)
    for h in range(n_heads):
        last = jnp.where(row_head == h, _shifted_bias(page + tok - key, table_ref, h), last)
        new = jnp.where(row_head == h, _shifted_bias(tok - key, table_ref, h), new)
    last_ref[...] = last
    new_ref[...] = jnp.where(key <= tok, new, NEG_INF)


def _sample_bias(table, n_heads, dec_seq, page):
    rows = 2 * n_heads * dec_seq
    return pl.pallas_call(
        functools.partial(_sample_bias_kernel, n_heads=n_heads, dec_seq=dec_seq, page=page),
        in_specs=[pl.BlockSpec(memory_space=pltpu.SMEM)],
        out_specs=[pl.BlockSpec(memory_space=pltpu.VMEM), pl.BlockSpec(memory_space=pltpu.VMEM)],
        out_shape=[jax.ShapeDtypeStruct((rows, page), F32), jax.ShapeDtypeStruct((rows, page), F32)],
        name="sample_bias",
    )(table)


def _attn_prompt_kernel(qt_ref, k_ref, vt_ref, bias_ref, lam_ref, sg_ref, o_ref,
                        qs_scr, m_scr, l_scr, acc_scr, *, layer):
    i = pl.program_id(2)
    t = qt_ref.shape[2]
    hd = QK_HEAD_DIM
    dv = V_HEAD_DIM
    heads = qs_scr.shape[0]
    zero = jnp.zeros((hd, t), BF16)
    for h in range(heads):
        qs_scr[h, 0:hd, 0:t] = qt_ref[0, h * dv:h * dv + hd, :]
        qs_scr[h, hd:2 * hd, 0:t] = zero
        qs_scr[h, 0:hd, t:2 * t] = zero
        qs_scr[h, hd:2 * hd, t:2 * t] = qt_ref[0, h * dv + hd:(h + 1) * dv, :]
    m_scr[...] = jnp.full(m_scr.shape, NEG_INF, F32)
    l_scr[...] = jnp.zeros(l_scr.shape, F32)
    acc_scr[...] = jnp.zeros(acc_scr.shape, F32)

    far, prev, diag = 0, 1, 2
    qc = MXU_DIM
    per_map = t // qc
    chunks = [(h, c) for c in range(2 * per_map) for h in range(heads)]

    sub = LANES
    masked_tile = jnp.full((sub, sub), NEG_INF, F32)

    def with_tiles(s, tiles):
        out_rows = []
        for a in range(s.shape[0] // sub):
            row = s[a * sub:(a + 1) * sub]
            if any(ta == a for ta, _ in tiles):
                pieces = []
                for cb in range(qc // sub):
                    piece = row[:, cb * sub:(cb + 1) * sub]
                    if (a, cb) in tiles:
                        tile = tiles[(a, cb)]
                        piece = masked_tile if tile is None else piece + tile
                    pieces.append(piece)
                row = jnp.concatenate(pieces, axis=1)
            out_rows.append(row)
        return jnp.concatenate(out_rows, axis=0)

    def run(blocks):
        seq = [(bi, h, c) for bi in range(len(blocks)) for (h, c) in chunks]

        def n_keys(bi, c):
            return (c % per_map + 1) * qc if blocks[bi][1] == diag else t

        def scores(bi, h, c):
            j, kind = blocks[bi]
            nk, cq = n_keys(bi, c), c % per_map
            k = k_ref[pl.ds(pl.multiple_of(j * t, t), nk), h * dv:(h + 1) * dv]
            s = jnp.dot(k, qs_scr[h, :, c * qc:(c + 1) * qc], preferred_element_type=F32)
            first = cq * (qc // sub)
            if kind == diag:
                tiles = {}
                for cb in range(qc // sub):
                    tiles[(first + cb, cb)] = bias_ref[h, 0]
                    if first + cb >= 1:
                        tiles[(first + cb - 1, cb)] = bias_ref[h, 1]
                    for a in range(first + cb + 1, nk // sub):
                        tiles[(a, cb)] = None
                s = with_tiles(s, tiles)
            elif kind == prev and cq == 0:
                s = with_tiles(s, {(t // sub - 1, 0): bias_ref[h, 1]})
            return s

        def softmax_update(h, c, s):
            cols = slice(c * qc, (c + 1) * qc)
            m_prev = m_scr[h, :, cols]
            m_new = jnp.maximum(m_prev, jnp.max(s, axis=0, keepdims=True))
            alpha = jnp.exp2(m_prev - m_new)
            p = jnp.exp2(s - m_new)
            l_scr[h, :, cols] = alpha * l_scr[h, :, cols] + jnp.sum(p, axis=0, keepdims=True)
            m_scr[h, :, cols] = m_new
            return p.astype(BF16), alpha

        def weigh(bi, h, c, p, alpha):
            cols = slice(c * qc, (c + 1) * qc)
            acc_scr[h, :, cols] = alpha * acc_scr[h, :, cols] + jnp.dot(
                vt_ref[blocks[bi][0], h * dv:(h + 1) * dv, 0:n_keys(bi, c)], p,
                preferred_element_type=F32)

        ahead = ATTN_SCORES_AHEAD
        s = {n: scores(*seq[n]) for n in range(min(ahead, len(seq)))}
        for n, (bi, h, c) in enumerate(seq):
            p, alpha = softmax_update(h, c, s.pop(n))
            if n + ahead < len(seq):
                s[n + ahead] = scores(*seq[n + ahead])
            weigh(bi, h, c, p, alpha)

    n_far = i - 1

    def far_pair(jj, carry):
        run([(2 * jj, far), (2 * jj + 1, far)])
        return carry

    lax.fori_loop(0, lax.shift_right_arithmetic(n_far, 1), far_pair, 0)

    @pl.when((n_far >= 1) & ((n_far & 1) == 1))
    def _():
        run([(n_far - 1, far)])

    @pl.when(i >= 1)
    def _():
        run([(i - 1, prev), (i, diag)])

    @pl.when(i == 0)
    def _():
        run([(i, diag)])

    lam = _diff_lambda_value(lam_ref, layer)
    for h in range(heads):
        o = acc_scr[h] / l_scr[h]
        od = o[:, 0:t] - lam * o[:, t:2 * t]
        on = od * lax.rsqrt(jnp.mean(od * od, axis=0, keepdims=True) + SUBLN_EPS)
        o_ref[:, h * dv:(h + 1) * dv] = (on.T * sg_ref[...] * (1.0 - _lambda_init(layer))).astype(BF16)


def _attn_prompt(qt, kb, vt, bias, lam, subln, batch, layer):
    n_tiles, d, tile = qt.shape
    m = kb.shape[0]
    seq = m // batch
    nq = seq // tile
    heads = ATTN_HEADS_PER_STEP
    width = heads * V_HEAD_DIM
    assert d % width == 0
    vmem = 4 * seq * width * 2 + 16 * tile * MXU_DIM * 4 + 4 * width * 2 * tile * 4
    return pl.pallas_call(
        functools.partial(_attn_prompt_kernel, layer=layer),
        grid=(batch, d // width, nq),
        in_specs=[pl.BlockSpec((1, width, tile), lambda b, g, i: (b * nq + i, g, 0)),
                  pl.BlockSpec((seq, width), lambda b, g, i: (b, g)),
                  pl.BlockSpec((nq, width, tile), lambda b, g, i: (b, g, 0)),
                  pl.BlockSpec((heads, 2, LANES, LANES), lambda b, g, i: (g, 0, 0, 0)),
                  pl.BlockSpec(lam.shape, lambda b, g, i: (0, 0)),
                  pl.BlockSpec((1, V_HEAD_DIM), lambda b, g, i: (0, 0))],
        out_specs=pl.BlockSpec((tile, width), lambda b, g, i: (b * nq + i, g)),
        out_shape=jax.ShapeDtypeStruct((m, d), BF16),
        scratch_shapes=[pltpu.VMEM((heads, V_HEAD_DIM, 2 * tile), BF16),
                        pltpu.VMEM((heads, 1, 2 * tile), F32),
                        pltpu.VMEM((heads, 1, 2 * tile), F32),
                        pltpu.VMEM((heads, V_HEAD_DIM, 2 * tile), F32)],
        compiler_params=_params(("parallel", "parallel", "arbitrary"), vmem),
        name="attn_prompt",
    )(qt, kb, vt, bias, lam, subln[None, :])


def _attn_sample_kernel(pt_ref, q_ref, kn_ref, vn_ref, kc_hbm, vc_hbm, blast_ref, bnew_ref, lam_ref,
                        sg_ref, o_ref, qbd_scr, lg_scr, w_scr, wn_scr, acc_scr, q8_scr, kn_scr, vn_scr,
                        kbuf, vbuf, ksem, vsem, *, layer, n_pages, dec_seq, slots):
    b = pl.program_id(0)
    s = pl.program_id(1)
    d = q_ref.shape[2]
    n_heads = d // V_HEAD_DIM
    v_parts = n_heads // VALUE_TILE_HEADS
    k_steps = n_pages // slots
    rows = 2 * n_heads * dec_seq
    half = rows // 2
    page = kbuf.shape[3]
    group_rows = acc_scr.shape[1]
    group_heads = group_rows // dec_seq

    def key_copies(bb, grp):
        return [pltpu.make_async_copy(kc_hbm.at[pt_ref[bb, grp * slots + e]], kbuf.at[grp, e], ksem.at[grp])
                for e in range(slots)]

    def value_copies(bb, grp):
        return [pltpu.make_async_copy(
                    vc_hbm.at[pt_ref[bb, grp * slots + e], :, pl.ds(part * VALUE_TILE_HEADS, VALUE_TILE_HEADS), :],
                    vbuf.at[grp, e, part], vsem.at[grp])
                for e in range(slots) for part in range(v_parts)]

    @pl.when((b == 0) & (s == 0))
    def _():
        for grp in range(k_steps):
            for cp in key_copies(0, grp):
                cp.start()

    @pl.when(s < k_steps)
    def _():
        for cp in value_copies(b, s):
            cp.start()

    @pl.when((s >= k_steps) & (b + 1 < pl.num_programs(0)))
    def _():
        for cp in key_copies(b + 1, s - k_steps):
            cp.start()

    @pl.when(s == 0)
    def _():
        kn_scr[...] = jnp.zeros(kn_scr.shape, F32)
        vn_scr[...] = jnp.zeros(vn_scr.shape, F32)
        kn_scr[0:dec_seq, :] = kn_ref[0]
        vn_scr[0:dec_seq, :] = vn_ref[0]
        for r in range(q8_scr.shape[0] // dec_seq):
            q8_scr[r * dec_seq:(r + 1) * dec_seq, :] = q_ref[0]
        q = q8_scr[...]
        qrep = jnp.broadcast_to(q[None], (rows // q.shape[0], q.shape[0], d)).reshape(rows, d)
        ridx = lax.broadcasted_iota(jnp.int32, (rows, d), 0)
        lane = lax.broadcasted_iota(jnp.int32, (rows, d), 1)
        row_head = (ridx % half) // dec_seq
        row_map = ridx // half
        match = (lane // V_HEAD_DIM == row_head) & ((lane // QK_HEAD_DIM) % 2 == row_map)
        qbd_scr[...] = jnp.where(match, qrep, 0.0).astype(BF16)
        acc_scr[...] = jnp.zeros(acc_scr.shape, F32)

    @pl.when(s < k_steps)
    def _():
        for cp in key_copies(b, s):
            cp.wait()
        for e in range(slots):
            lg_scr[s * slots + e] = jnp.dot(qbd_scr[...], kbuf[s, e].astype(BF16),
                                            preferred_element_type=F32)

    @pl.when(s == k_steps - 1)
    def _():
        lam = _diff_lambda_value(lam_ref, layer)
        lg_scr[n_pages - 1] = lg_scr[n_pages - 1] + blast_ref[...]
        lgn = lax.dot_general(qbd_scr[...], kn_scr[...].astype(BF16), (((1,), (1,)), ((), ())),
                              preferred_element_type=F32) + bnew_ref[...]
        lg = lg_scr[...]
        m = jnp.maximum(jnp.max(jnp.max(lg, axis=0), axis=-1, keepdims=True),
                        jnp.max(lgn, axis=-1, keepdims=True))
        p = jnp.exp(lg - m[None])
        pn = jnp.exp(lgn - m)
        l = jnp.sum(jnp.sum(p, axis=0), axis=-1, keepdims=True) + jnp.sum(pn, axis=-1, keepdims=True)
        inv = 1.0 / l
        p = p * inv[None]
        pn = pn * inv
        w_scr[...] = p[:, 0:half, :] - lam * p[:, half:rows, :]
        wn_scr[...] = (pn[0:half, :] - lam * pn[half:rows, :]).astype(BF16)

    @pl.when(s >= k_steps)
    def _():
        grp = s - k_steps
        for cp in value_copies(b, grp):
            cp.wait()
        for g in range(n_heads // group_heads):
            part, first = divmod(g * group_heads, VALUE_TILE_HEADS)
            acc = acc_scr[g]
            for e in range(slots):
                vflat = vbuf.at[grp, e, part].reshape(page * VALUE_TILE_HEADS, V_HEAD_DIM)
                v = jnp.concatenate(
                    [vflat[pl.ds(first + hh, page, stride=VALUE_TILE_HEADS), :].astype(BF16)
                     for hh in range(group_heads)], axis=1)
                w = w_scr[grp * slots + e, g * group_rows:(g + 1) * group_rows, :]
                acc = acc + jnp.dot(w.astype(BF16), v, preferred_element_type=F32)
            acc_scr[g] = acc

    @pl.when(s == 2 * k_steps - 1)
    def _():
        new = jnp.dot(wn_scr[...], vn_scr[...].astype(BF16), preferred_element_type=F32)
        scale = 1.0 - _lambda_init(layer)
        for h in range(n_heads):
            g, e = divmod(h, group_heads)
            cols = slice(h * V_HEAD_DIM, (h + 1) * V_HEAD_DIM)
            piece = (acc_scr[g, e * dec_seq:(e + 1) * dec_seq, e * V_HEAD_DIM:(e + 1) * V_HEAD_DIM]
                     + new[h * dec_seq:(h + 1) * dec_seq, cols])
            o_ref[0, :, cols] = _rms_rows(piece, sg_ref[...], SUBLN_EPS) * scale


def _attn_sample(q, k_new, v_new, cache_k, cache_v, page_table, blast, bnew, lam, subln, layer):
    batch, dec_seq, d = q.shape
    n_pages = page_table.shape[1]
    page = cache_k.shape[2]
    n_heads = d // V_HEAD_DIM
    rows = 2 * n_heads * dec_seq
    sub_f32 = 8
    assert sub_f32 % dec_seq == 0
    tok = pl.BlockSpec((1, dec_seq, d), lambda b, s, pt: (b, 0, 0))
    full = lambda a: pl.BlockSpec(a.shape, lambda b, s, pt: (0,) * a.ndim)
    slots = SAMPLE_PAGE_SLOTS
    assert n_pages % slots == 0
    k_steps = n_pages // slots
    v_parts = n_heads // VALUE_TILE_HEADS
    grid_spec = pltpu.PrefetchScalarGridSpec(
        num_scalar_prefetch=1,
        grid=(batch, 2 * k_steps),
        in_specs=[tok, tok, tok,
                  pl.BlockSpec(memory_space=pl.ANY), pl.BlockSpec(memory_space=pl.ANY),
                  full(blast), full(bnew), full(lam),
                  pl.BlockSpec((1, V_HEAD_DIM), lambda b, s, pt: (0, 0))],
        out_specs=pl.BlockSpec((1, dec_seq, d), lambda b, s, pt: (b, 0, 0)),
        scratch_shapes=[pltpu.VMEM((rows, d), BF16),
                        pltpu.VMEM((n_pages, rows, page), F32),
                        pltpu.VMEM((n_pages, rows // 2, page), F32),
                        pltpu.VMEM((rows // 2, page), BF16),
                        pltpu.VMEM((n_heads * dec_seq // sub_f32, sub_f32,
                                    sub_f32 // dec_seq * V_HEAD_DIM), F32),
                        pltpu.VMEM((sub_f32, d), F32),
                        pltpu.VMEM((page, d), F32),
                        pltpu.VMEM((page, d), F32),
                        pltpu.VMEM((k_steps, slots, d, page), F32),
                        pltpu.VMEM((k_steps, slots, v_parts, page, VALUE_TILE_HEADS, V_HEAD_DIM), F32),
                        pltpu.SemaphoreType.DMA((k_steps,)),
                        pltpu.SemaphoreType.DMA((k_steps,))],
    )
    vmem = 2 * k_steps * slots * page * d * 4 + 6 * n_pages * rows * page * 4 + 4 * page * d * 2
    return pl.pallas_call(
        functools.partial(_attn_sample_kernel, layer=layer, n_pages=n_pages, dec_seq=dec_seq, slots=slots),
        grid_spec=grid_spec,
        out_shape=jax.ShapeDtypeStruct((batch, dec_seq, d), F32),
        compiler_params=_params(("arbitrary", "arbitrary"), vmem),
        name="attn_sample",
    )(page_table, q, k_new, v_new, cache_k, cache_v, blast, bnew, lam, subln[None, :])


PROMPT_TM = 512
ATTN_TILE = 512
PROJ_TN = 512
RES_TN = 2048
FFN_TF = 1024
FFN_CAST_TF = 512
SAMPLE_PAGE_SLOTS = 8
ATTN_HEADS_PER_STEP = 4
ATTN_SCORES_AHEAD = 4


def kernel(x_prompt, x_sample, cache_k, cache_v, page_table, rel_bias_table, mix_norm, ffn_norm,
           attn_w_qkv, attn_w_o, attn_q_norm, attn_k_norm, attn_lambda, attn_subln,
           cm_w_in, cm_v_norm, cm_w_s, cm_b_s, cm_w_o, ffn_w1, ffn_w2):
    batch, seq, d = x_prompt.shape
    dec_batch, dec_seq, _ = x_sample.shape
    depth = mix_norm.shape[0]
    n_heads = d // V_HEAD_DIM
    n_groups = cm_w_s.shape[1]
    page = cache_k.shape[2]
    mp, ms = batch * seq, dec_batch * dec_seq
    assert seq % ATTN_TILE == 0 and mp % PROMPT_TM == 0 and ms % CHUNK == 0 and CHUNK % dec_seq == 0

    xp = x_prompt.reshape(mp, d)
    xs = x_sample.reshape(ms, d)
    ck = jnp.transpose(cache_k, (0, 1, 3, 4, 5, 2)).reshape(cache_k.shape[0], cache_k.shape[1], d, page)
    cv = cache_v

    bias_p = _prompt_bias(rel_bias_table, n_heads)
    blast, bnew = _sample_bias(rel_bias_table, n_heads, dec_seq, page)

    k_prompt, v_prompt, k_sample, v_sample, cv_sample = [], [], [], [], []
    for i in range(depth):
        j = i // N_MIXERS
        if i % N_MIXERS == 0:
            w_qkv = attn_w_qkv[j].astype(BF16)
            w_o = attn_w_o[j].astype(BF16)
            qtp, kbp, ktp, vp, vtp = _qkv(xp, mix_norm[i], w_qkv, attn_q_norm[j], attn_k_norm[j],
                                          ATTN_TILE, PROJ_TN, seq=seq)
            qs, ks, vs = _qkv(xs, mix_norm[i], w_qkv, attn_q_norm[j], attn_k_norm[j], ms, PROJ_TN)
            op = _attn_prompt(qtp, kbp, vtp, bias_p, attn_lambda[j], attn_subln[j], batch, i)
            os_ = _attn_sample(qs.reshape(dec_batch, dec_seq, d),
                               ks.reshape(dec_batch, dec_seq, d), vs.reshape(dec_batch, dec_seq, d),
                               ck[j], cv[j], page_table, blast, bnew, attn_lambda[j], attn_subln[j], i)
            xp = _mm_res(xp, op, w_o, PROMPT_TM, RES_TN)
            xs = _mm_res(xs, os_.reshape(ms, d), w_o, ms, RES_TN)
            k_prompt.append(jnp.transpose(ktp.reshape(batch, n_heads, 2, QK_HEAD_DIM, seq), (0, 4, 1, 2, 3)))
            v_prompt.append(vp.reshape(batch, seq, n_heads, V_HEAD_DIM))
            k_sample.append(ks.reshape(dec_batch, dec_seq, n_heads, 2, QK_HEAD_DIM))
            v_sample.append(vs.reshape(dec_batch, dec_seq, n_heads, V_HEAD_DIM))
        else:
            w_in = cm_w_in[j].astype(BF16)
            w_o = cm_w_o[j].astype(BF16)
            bs_p = jnp.broadcast_to(cm_b_s[j][:, :, None], (n_groups, CHUNK, GROUP_DIM))
            gp, _ = _gmlp_front(xp, mix_norm[i], w_in, cm_v_norm[j], cm_w_s[j], bs_p,
                                PROMPT_TM, PROJ_TN, False)
            reps = CHUNK // dec_seq
            eye = jnp.eye(reps, dtype=F32)
            ws_s = (eye[None, :, None, :, None] * cm_w_s[j][:, None, :dec_seq, None, :dec_seq]
                    ).reshape(n_groups, CHUNK, CHUNK)
            bs_s = jnp.broadcast_to(jnp.tile(cm_b_s[j][:, :dec_seq], (1, reps))[:, :, None],
                                    (n_groups, CHUNK, GROUP_DIM))
            gs, cvs = _gmlp_front(xs, mix_norm[i], w_in, cm_v_norm[j], ws_s, bs_s, ms, PROJ_TN, True)
            xp = _mm_res(xp, gp, w_o, PROMPT_TM, RES_TN)
            xs = _mm_res(xs, gs, w_o, ms, RES_TN)
            cv_sample.append(cvs.reshape(dec_batch, dec_seq, n_groups, GROUP_DIM))
        xs, w1, w2 = _ffn(xs, ffn_norm[i], ffn_w1, ffn_w2, ms, FFN_CAST_TF, layer=i)
        xp = _ffn(xp, ffn_norm[i], w1, w2, PROMPT_TM, FFN_TF)

    return (xp.reshape(batch, seq, d), xs.reshape(dec_batch, dec_seq, d),
            jnp.stack(k_prompt), jnp.stack(v_prompt), jnp.stack(k_sample), jnp.stack(v_sample),
            jnp.stack(cv_sample))
```

```python
import functools
import math

import jax
import jax.numpy as jnp
from jax import lax
from jax.experimental import pallas as pl
from jax.experimental.pallas import tpu as pltpu

F32 = jnp.float32
BF16 = jnp.bfloat16

QK_HEAD_DIM = 64
V_HEAD_DIM = 128
NUM_BUCKETS = 32
MAX_DISTANCE = 128
CHUNK = 128
GROUP_DIM = 128
N_MIXERS = 2
NORM_EPS = 1e-6
SUBLN_EPS = 1e-5
NEG_INF = -1e30
LOG2_E = 1.4426950408889634

LANES = 128
VALUE_TILE_HEADS = 8
MXU_DIM = 256
VMEM_CAP = 56 * 1024 * 1024


def _lambda_init(layer):
    return 0.8 - 0.6 * math.exp(-0.3 * layer)


def _params(semantics, vmem_bytes):
    limit = int(min(VMEM_CAP, max(32 * 1024 * 1024, vmem_bytes * 5 // 4)))
    return pltpu.CompilerParams(dimension_semantics=semantics, vmem_limit_bytes=limit)


def _rms_rows(x, g, eps):
    ms = jnp.mean(x * x, axis=-1, keepdims=True)
    return x * lax.rsqrt(ms + eps) * g


def _group_rms(x, bsum, group, eps):
    x2 = x * x
    hi = x2.astype(BF16)
    lo = (x2 - hi.astype(F32)).astype(BF16)
    parts = []
    for c in range(x.shape[1] // MXU_DIM):
        sl = slice(c * MXU_DIM, (c + 1) * MXU_DIM)
        parts.append(jnp.dot(hi[:, sl], bsum, preferred_element_type=F32)
                     + jnp.dot(lo[:, sl], bsum, preferred_element_type=F32))
    ss = parts[0] if len(parts) == 1 else jnp.concatenate(parts, axis=1)
    return x * lax.rsqrt(ss * (1.0 / group) + eps)


def _group_sum_matrix(group):
    r = jnp.arange(MXU_DIM) // group
    return (r[:, None] == r[None, :]).astype(BF16)


def _diff_lambda_value(lam_ref, layer):
    lf = lam_ref[...]
    a = jnp.sum(lf[0:1] * lf[1:2], axis=-1, keepdims=True)
    b = jnp.sum(lf[2:3] * lf[3:4], axis=-1, keepdims=True)
    return jnp.exp(a) - jnp.exp(b) + _lambda_init(layer)


def _qkv_kernel(x_ref, g_ref, wq_ref, wk_ref, wv_ref, qg_ref, kg_ref, bsum_ref, *rest, feature_major):
    if feature_major:
        qt_ref, kb_ref, kt_ref, v_ref, vt_ref, h_scr = rest
    else:
        q_ref, k_ref, v_ref, h_scr = rest

    @pl.when(pl.program_id(1) == 0)
    def _():
        h_scr[...] = _rms_rows(x_ref[...], g_ref[...], NORM_EPS).astype(BF16)

    h = h_scr[...]
    bsum = bsum_ref[...]
    q_scale = QK_HEAD_DIM ** -0.5 * (LOG2_E if feature_major else 1.0)
    q = jnp.dot(h, wq_ref[...], preferred_element_type=F32)
    qn = _group_rms(q, bsum, QK_HEAD_DIM, NORM_EPS) * qg_ref[...] * q_scale
    k = jnp.dot(h, wk_ref[...], preferred_element_type=F32)
    kn = _group_rms(k, bsum, QK_HEAD_DIM, NORM_EPS) * kg_ref[...]
    v = jnp.dot(h, wv_ref[...], preferred_element_type=F32)
    v_ref[...] = v
    if feature_major:
        qt_ref[0] = qn.T.astype(BF16)
        kb_ref[...] = kn.astype(BF16)
        kt_ref[0] = kn.T
        vt_ref[0] = v.T.astype(BF16)
    else:
        q_ref[...] = qn
        k_ref[...] = kn


def _qkv(x, g, w_qkv, q_gain, k_gain, tm, tn, seq=None):
    m, d = x.shape
    n = w_qkv.shape[1] // 3
    nj = n // tn
    qg = jnp.tile(q_gain, tn // QK_HEAD_DIM)[None, :]
    kg = jnp.tile(k_gain, tn // QK_HEAD_DIM)[None, :]
    bsum = _group_sum_matrix(QK_HEAD_DIM)
    row = pl.BlockSpec((tm, d), lambda i, j: (i, 0))
    col = lambda off: pl.BlockSpec((d, tn), lambda i, j, off=off: (0, j + off))
    vec = lambda width: pl.BlockSpec((1, width), lambda i, j: (0, 0))
    out = pl.BlockSpec((tm, tn), lambda i, j: (i, j))
    if seq is None:
        out_specs = [out, out, out]
        out_shape = [jax.ShapeDtypeStruct((m, n), F32)] * 3
    else:
        per_seq = seq // tm
        tile_t = pl.BlockSpec((1, tn, tm), lambda i, j: (i, j, 0))
        seq_t = pl.BlockSpec((1, tn, tm), lambda i, j: (i // per_seq, j, i % per_seq))
        out_specs = [tile_t, out, seq_t, out, tile_t]
        out_shape = [jax.ShapeDtypeStruct((m // tm, n, tm), BF16), jax.ShapeDtypeStruct((m, n), BF16),
                     jax.ShapeDtypeStruct((m // seq, n, seq), F32), jax.ShapeDtypeStruct((m, n), F32),
                     jax.ShapeDtypeStruct((m // tm, n, tm), BF16)]
    vmem = 2 * tm * d * 4 + tm * d * 2 + 6 * d * tn * 2 + 2 * tm * tn * 16 + 10 * tm * tn * 4
    return pl.pallas_call(
        functools.partial(_qkv_kernel, feature_major=seq is not None),
        grid=(m // tm, nj),
        in_specs=[row, vec(d), col(0), col(nj), col(2 * nj), vec(tn), vec(tn),
                  pl.BlockSpec((MXU_DIM, MXU_DIM), lambda i, j: (0, 0))],
        out_specs=out_specs,
        out_shape=out_shape,
        scratch_shapes=[pltpu.VMEM((tm, d), BF16)],
        compiler_params=_params(("parallel", "arbitrary"), vmem),
        name="qkv",
    )(x, g[None, :], w_qkv, w_qkv, w_qkv, qg, kg, bsum)


def _mm_res_kernel(x_ref, a_ref, w_ref, o_ref):
    o_ref[...] = x_ref[...] + jnp.dot(a_ref[...].astype(BF16), w_ref[...],
                                      preferred_element_type=F32)


def _mm_res(x, a, w, tm, tn):
    m, n = x.shape
    kd = a.shape[1]
    vmem = 2 * tm * kd * a.dtype.itemsize + 2 * kd * tn * 2 + 4 * tm * tn * 4 + tm * kd * 2 + tm * tn * 4
    return pl.pallas_call(
        _mm_res_kernel,
        grid=(m // tm, n // tn),
        in_specs=[pl.BlockSpec((tm, tn), lambda i, j: (i, j)),
                  pl.BlockSpec((tm, kd), lambda i, j: (i, 0)),
                  pl.BlockSpec((kd, tn), lambda i, j: (0, j))],
        out_specs=pl.BlockSpec((tm, tn), lambda i, j: (i, j)),
        out_shape=jax.ShapeDtypeStruct((m, n), F32),
        compiler_params=_params(("parallel", "arbitrary"), vmem),
        name="mm_res",
    )(x, a, w)


def _ffn_kernel(x_ref, g_ref, w1_ref, w2_ref, o_ref, *rest, emit_weights):
    if emit_weights:
        w1b_ref, w2b_ref, h_scr = rest
    else:
        (h_scr,) = rest

    @pl.when(pl.program_id(1) == 0)
    def _():
        x = x_ref[...]
        h_scr[...] = _rms_rows(x, g_ref[...], NORM_EPS).astype(BF16)
        o_ref[...] = x

    w1 = w1_ref[...].astype(BF16)
    w2 = w2_ref[...].astype(BF16)
    if emit_weights:
        w1b_ref[...] = w1
        w2b_ref[...] = w2
    a = jnp.dot(h_scr[...], w1, preferred_element_type=F32)
    a = jnp.maximum(a, 0.0)
    a = (a * a).astype(BF16)
    o_ref[...] += jnp.dot(a, w2, preferred_element_type=F32)


def _ffn(x, g, w1, w2, tm, tf, layer=None):
    m, d = x.shape
    dff = w1.shape[-1]
    emit = layer is not None
    assert not emit or m == tm
    if emit:
        w1_spec = pl.BlockSpec((None, d, tf), lambda i, f: (layer, 0, f))
        w2_spec = pl.BlockSpec((None, tf, d), lambda i, f: (layer, f, 0))
    else:
        w1_spec = pl.BlockSpec((d, tf), lambda i, f: (0, f))
        w2_spec = pl.BlockSpec((tf, d), lambda i, f: (f, 0))
    out_specs = [pl.BlockSpec((tm, d), lambda i, f: (i, 0))]
    out_shape = [jax.ShapeDtypeStruct((m, d), F32)]
    if emit:
        out_specs += [pl.BlockSpec((d, tf), lambda i, f: (0, f)), pl.BlockSpec((tf, d), lambda i, f: (f, 0))]
        out_shape += [jax.ShapeDtypeStruct((d, dff), BF16), jax.ShapeDtypeStruct((dff, d), BF16)]
    vmem = (4 * tm * d * 4 + tm * d * 2 + 4 * d * tf * w1.dtype.itemsize + tm * tf * 8
            + (6 * d * tf * 2 if emit else 0))
    res = pl.pallas_call(
        functools.partial(_ffn_kernel, emit_weights=emit),
        grid=(m // tm, dff // tf),
        in_specs=[pl.BlockSpec((tm, d), lambda i, f: (i, 0)),
                  pl.BlockSpec((1, d), lambda i, f: (0, 0)),
                  w1_spec, w2_spec],
        out_specs=out_specs,
        out_shape=out_shape,
        scratch_shapes=[pltpu.VMEM((tm, d), BF16)],
        compiler_params=_params(("parallel", "arbitrary"), vmem),
        name="ffn",
    )(x, g[None, :], w1, w2)
    return res if emit else res[0]


def _gelu(x):
    return 0.5 * x * (1.0 + lax.erf(x * (2.0 ** -0.5)))


def _gmlp_kernel(x_ref, g_ref, wu_ref, wv_ref, vg_ref, bsum_ref, ws_ref, bs_ref, *rest, emit_cv):
    if emit_cv:
        gate_ref, cv_ref, h_scr = rest
    else:
        gate_ref, h_scr = rest

    @pl.when(pl.program_id(1) == 0)
    def _():
        h_scr[...] = _rms_rows(x_ref[...], g_ref[...], NORM_EPS).astype(BF16)

    h = h_scr[...]
    tm = h.shape[0]
    tn = wu_ref.shape[1]
    u = _gelu(jnp.dot(h, wu_ref[...], preferred_element_type=F32))
    v = _gelu(jnp.dot(h, wv_ref[...], preferred_element_type=F32))
    vn = _group_rms(v, bsum_ref[...], GROUP_DIM, NORM_EPS) * vg_ref[...]
    if emit_cv:
        cv_ref[...] = vn
    vb = vn.astype(BF16)
    t_idx = lax.broadcasted_iota(jnp.int32, (CHUNK, CHUNK), 0)
    s_idx = lax.broadcasted_iota(jnp.int32, (CHUNK, CHUNK), 1)
    causal = t_idx >= s_idx
    for gi in range(tn // GROUP_DIM):
        cols = slice(gi * GROUP_DIM, (gi + 1) * GROUP_DIM)
        w = jnp.where(causal, ws_ref[gi], 0.0).astype(BF16)
        bias = bs_ref[gi]
        for c in range(tm // CHUNK):
            rows = slice(c * CHUNK, (c + 1) * CHUNK)
            mixed = jnp.dot(w, vb[rows, cols], preferred_element_type=F32) + bias
            gate_ref[rows, cols] = (u[rows, cols] * mixed).astype(BF16)


def _gmlp_front(x, g, w_in, v_gain, ws, bs, tm, tn, emit_cv):
    m, d = x.shape
    dg = w_in.shape[1] // 2
    nj = dg // tn
    gpt = tn // GROUP_DIM
    blk = pl.BlockSpec((tm, tn), lambda i, j: (i, j))
    out_specs = [blk, blk] if emit_cv else [blk]
    out_shape = [jax.ShapeDtypeStruct((m, dg), BF16)]
    if emit_cv:
        out_shape.append(jax.ShapeDtypeStruct((m, dg), F32))
    vmem = 2 * tm * d * 4 + tm * d * 2 + 4 * d * tn * 2 + 4 * gpt * CHUNK * CHUNK * 4 + 12 * tm * tn * 4
    res = pl.pallas_call(
        functools.partial(_gmlp_kernel, emit_cv=emit_cv),
        grid=(m // tm, nj),
        in_specs=[pl.BlockSpec((tm, d), lambda i, j: (i, 0)),
                  pl.BlockSpec((1, d), lambda i, j: (0, 0)),
                  pl.BlockSpec((d, tn), lambda i, j: (0, j)),
                  pl.BlockSpec((d, tn), lambda i, j, nj=nj: (0, j + nj)),
                  pl.BlockSpec((1, tn), lambda i, j: (0, j)),
                  pl.BlockSpec((MXU_DIM, MXU_DIM), lambda i, j: (0, 0)),
                  pl.BlockSpec((gpt, CHUNK, CHUNK), lambda i, j: (j, 0, 0)),
                  pl.BlockSpec((gpt, CHUNK, GROUP_DIM), lambda i, j: (j, 0, 0))],
        out_specs=out_specs,
        out_shape=out_shape,
        scratch_shapes=[pltpu.VMEM((tm, d), BF16)],
        compiler_params=_params(("parallel", "arbitrary"), vmem),
        name="gmlp_front",
    )(x, g[None, :], w_in, w_in, v_gain[None, :], _group_sum_matrix(GROUP_DIM), ws, bs)
    return res if emit_cv else (res[0], None)


def _shifted_bias(dist, table_ref, head):
    max_exact = NUM_BUCKETS // 2
    n = jnp.maximum(dist, 0)
    nf = jnp.maximum(n, 1).astype(F32)
    large = max_exact + (jnp.log(nf / max_exact) / math.log(MAX_DISTANCE / max_exact)
                         * (NUM_BUCKETS - max_exact)).astype(jnp.int32)
    bucket = jnp.where(n < max_exact, n, jnp.minimum(large, NUM_BUCKETS - 1))
    far = table_ref[NUM_BUCKETS - 1, head]
    acc = jnp.zeros(dist.shape, F32)
    for b in range(NUM_BUCKETS - 1):
        acc = jnp.where(bucket == b, table_ref[b, head] - far, acc)
    return acc


def _prompt_bias_kernel(table_ref, o_ref):
    head = pl.program_id(0)
    key = lax.broadcasted_iota(jnp.int32, (LANES, LANES), 0)
    qry = lax.broadcasted_iota(jnp.int32, (LANES, LANES), 1)
    o_ref[0, 0] = jnp.where(qry >= key, _shifted_bias(qry - key, table_ref, head) * LOG2_E, NEG_INF)
    o_ref[0, 1] = _shifted_bias(LANES + qry - key, table_ref, head) * LOG2_E


def _prompt_bias(table, n_heads):
    return pl.pallas_call(
        _prompt_bias_kernel,
        grid=(n_heads,),
        in_specs=[pl.BlockSpec(memory_space=pltpu.SMEM)],
        out_specs=pl.BlockSpec((1, 2, LANES, LANES), lambda h: (h, 0, 0, 0)),
        out_shape=jax.ShapeDtypeStruct((n_heads, 2, LANES, LANES), F32),
        name="prompt_bias",
    )(table)


def _sample_bias_kernel(table_ref, last_ref, new_ref, *, n_heads, dec_seq, page):
    rows = 2 * n_heads * dec_seq
    ridx = lax.broadcasted_iota(jnp.int32, (rows, page), 0)
    key = lax.broadcasted_iota(jnp.int32, (rows, page), 1)
    tok = ridx % dec_seq
    row_head = (ridx % (n_heads * dec_seq)) // dec_seq
    last = jnp.zeros((rows, page), F32)
    new = jnp.zeros((rows, page), F32)
    for h in range(n_heads):
        last = jnp.where(row_head == h, _shifted_bias(page + tok - key, table_ref, h), last)
        new = jnp.where(row_head == h, _shifted_bias(tok - key, table_ref, h), new)
    last_ref[...] = last
    new_ref[...] = jnp.where(key <= tok, new, NEG_INF)


def _sample_bias(table, n_heads, dec_seq, page):
    rows = 2 * n_heads * dec_seq
    return pl.pallas_call(
        functools.partial(_sample_bias_kernel, n_heads=n_heads, dec_seq=dec_seq, page=page),
        in_specs=[pl.BlockSpec(memory_space=pltpu.SMEM)],
        out_specs=[pl.BlockSpec(memory_space=pltpu.VMEM), pl.BlockSpec(memory_space=pltpu.VMEM)],
        out_shape=[jax.ShapeDtypeStruct((rows, page), F32), jax.ShapeDtypeStruct((rows, page), F32)],
        name="sample_bias",
    )(table)


def _attn_prompt_kernel(qt_ref, k_ref, vt_ref, bias_ref, lam_ref, sg_ref, o_ref,
                        qs_scr, m_scr, l_scr, acc_scr, *, layer):
    i = pl.program_id(2)
    t = qt_ref.shape[2]
    hd = QK_HEAD_DIM
    dv = V_HEAD_DIM
    heads = qs_scr.shape[0]
    zero = jnp.zeros((hd, t), BF16)
    for h in range(heads):
        qs_scr[h, 0:hd, 0:t] = qt_ref[0, h * dv:h * dv + hd, :]
        qs_scr[h, hd:2 * hd, 0:t] = zero
        qs_scr[h, 0:hd, t:2 * t] = zero
        qs_scr[h, hd:2 * hd, t:2 * t] = qt_ref[0, h * dv + hd:(h + 1) * dv, :]
    m_scr[...] = jnp.full(m_scr.shape, NEG_INF, F32)
    l_scr[...] = jnp.zeros(l_scr.shape, F32)
    acc_scr[...] = jnp.zeros(acc_scr.shape, F32)

    far, prev, diag = 0, 1, 2
    qc = MXU_DIM
    per_map = t // qc
    chunks = [(h, c) for c in range(2 * per_map) for h in range(heads)]

    sub = LANES
    masked_tile = jnp.full((sub, sub), NEG_INF, F32)

    def with_tiles(s, tiles):
        out_rows = []
        for a in range(s.shape[0] // sub):
            row = s[a * sub:(a + 1) * sub]
            if any(ta == a for ta, _ in tiles):
                pieces = []
                for cb in range(qc // sub):
                    piece = row[:, cb * sub:(cb + 1) * sub]
                    if (a, cb) in tiles:
                        tile = tiles[(a, cb)]
                        piece = masked_tile if tile is None else piece + tile
                    pieces.append(piece)
                row = jnp.concatenate(pieces, axis=1)
            out_rows.append(row)
        return jnp.concatenate(out_rows, axis=0)

    def run(blocks):
        seq = [(bi, h, c) for bi in range(len(blocks)) for (h, c) in chunks]

        def n_keys(bi, c):
            return (c % per_map + 1) * qc if blocks[bi][1] == diag else t

        def scores(bi, h, c):
            j, kind = blocks[bi]
            nk, cq = n_keys(bi, c), c % per_map
            k = k_ref[pl.ds(pl.multiple_of(j * t, t), nk), h * dv:(h + 1) * dv]
            s = jnp.dot(k, qs_scr[h, :, c * qc:(c + 1) * qc], preferred_element_type=F32)
            first = cq * (qc // sub)
            if kind == diag:
                tiles = {}
                for cb in range(qc // sub):
                    tiles[(first + cb, cb)] = bias_ref[h, 0]
                    if first + cb >= 1:
                        tiles[(first + cb - 1, cb)] = bias_ref[h, 1]
                    for a in range(first + cb + 1, nk // sub):
                        tiles[(a, cb)] = None
                s = with_tiles(s, tiles)
            elif kind == prev and cq == 0:
                s = with_tiles(s, {(t // sub - 1, 0): bias_ref[h, 1]})
            return s

        def softmax_update(h, c, s):
            cols = slice(c * qc, (c + 1) * qc)
            m_prev = m_scr[h, :, cols]
            m_new = jnp.maximum(m_prev, jnp.max(s, axis=0, keepdims=True))
            alpha = jnp.exp2(m_prev - m_new)
            p = jnp.exp2(s - m_new)
            l_scr[h, :, cols] = alpha * l_scr[h, :, cols] + jnp.sum(p, axis=0, keepdims=True)
            m_scr[h, :, cols] = m_new
            return p.astype(BF16), alpha

        def weigh(bi, h, c, p, alpha):
            cols = slice(c * qc, (c + 1) * qc)
            acc_scr[h, :, cols] = alpha * acc_scr[h, :, cols] + jnp.dot(
                vt_ref[blocks[bi][0], h * dv:(h + 1) * dv, 0:n_keys(bi, c)], p,
                preferred_element_type=F32)

        ahead = ATTN_SCORES_AHEAD
        s = {n: scores(*seq[n]) for n in range(min(ahead, len(seq)))}
        for n, (bi, h, c) in enumerate(seq):
            p, alpha = softmax_update(h, c, s.pop(n))
            if n + ahead < len(seq):
                s[n + ahead] = scores(*seq[n + ahead])
            weigh(bi, h, c, p, alpha)

    n_far = i - 1

    def far_pair(jj, carry):
        run([(2 * jj, far), (2 * jj + 1, far)])
        return carry

    lax.fori_loop(0, lax.shift_right_arithmetic(n_far, 1), far_pair, 0)

    @pl.when((n_far >= 1) & ((n_far & 1) == 1))
    def _():
        run([(n_far - 1, far)])

    @pl.when(i >= 1)
    def _():
        run([(i - 1, prev), (i, diag)])

    @pl.when(i == 0)
    def _():
        run([(i, diag)])

    lam = _diff_lambda_value(lam_ref, layer)
    for h in range(heads):
        o = acc_scr[h] / l_scr[h]
        od = o[:, 0:t] - lam * o[:, t:2 * t]
        on = od * lax.rsqrt(jnp.mean(od * od, axis=0, keepdims=True) + SUBLN_EPS)
        o_ref[:, h * dv:(h + 1) * dv] = (on.T * sg_ref[...] * (1.0 - _lambda_init(layer))).astype(BF16)


def _attn_prompt(qt, kb, vt, bias, lam, subln, batch, layer):
    n_tiles, d, tile = qt.shape
    m = kb.shape[0]
    seq = m // batch
    nq = seq // tile
    heads = ATTN_HEADS_PER_STEP
    width = heads * V_HEAD_DIM
    assert d % width == 0
    vmem = 4 * seq * width * 2 + 16 * tile * MXU_DIM * 4 + 4 * width * 2 * tile * 4
    return pl.pallas_call(
        functools.partial(_attn_prompt_kernel, layer=layer),
        grid=(batch, d // width, nq),
        in_specs=[pl.BlockSpec((1, width, tile), lambda b, g, i: (b * nq + i, g, 0)),
                  pl.BlockSpec((seq, width), lambda b, g, i: (b, g)),
                  pl.BlockSpec((nq, width, tile), lambda b, g, i: (b, g, 0)),
                  pl.BlockSpec((heads, 2, LANES, LANES), lambda b, g, i: (g, 0, 0, 0)),
                  pl.BlockSpec(lam.shape, lambda b, g, i: (0, 0)),
                  pl.BlockSpec((1, V_HEAD_DIM), lambda b, g, i: (0, 0))],
        out_specs=pl.BlockSpec((tile, width), lambda b, g, i: (b * nq + i, g)),
        out_shape=jax.ShapeDtypeStruct((m, d), BF16),
        scratch_shapes=[pltpu.VMEM((heads, V_HEAD_DIM, 2 * tile), BF16),
                        pltpu.VMEM((heads, 1, 2 * tile), F32),
                        pltpu.VMEM((heads, 1, 2 * tile), F32),
                        pltpu.VMEM((heads, V_HEAD_DIM, 2 * tile), F32)],
        compiler_params=_params(("parallel", "parallel", "arbitrary"), vmem),
        name="attn_prompt",
    )(qt, kb, vt, bias, lam, subln[None, :])


def _attn_sample_kernel(pt_ref, q_ref, kn_ref, vn_ref, kc_hbm, vc_hbm, blast_ref, bnew_ref, lam_ref,
                        sg_ref, o_ref, qbd_scr, lg_scr, w_scr, wn_scr, acc_scr, q8_scr, kn_scr, vn_scr,
                        kbuf, vbuf, ksem, vsem, *, layer, n_pages, dec_seq, slots):
    b = pl.program_id(0)
    s = pl.program_id(1)
    d = q_ref.shape[2]
    n_heads = d // V_HEAD_DIM
    v_parts = n_heads // VALUE_TILE_HEADS
    k_steps = n_pages // slots
    rows = 2 * n_heads * dec_seq
    half = rows // 2
    page = kbuf.shape[3]
    group_rows = acc_scr.shape[1]
    group_heads = group_rows // dec_seq

    def key_copies(bb, grp):
        return [pltpu.make_async_copy(kc_hbm.at[pt_ref[bb, grp * slots + e]], kbuf.at[grp, e], ksem.at[grp])
                for e in range(slots)]

    def value_copies(bb, grp):
        return [pltpu.make_async_copy(
                    vc_hbm.at[pt_ref[bb, grp * slots + e], :, pl.ds(part * VALUE_TILE_HEADS, VALUE_TILE_HEADS), :],
                    vbuf.at[grp, e, part], vsem.at[grp])
                for e in range(slots) for part in range(v_parts)]

    @pl.when((b == 0) & (s == 0))
    def _():
        for grp in range(k_steps):
            for cp in key_copies(0, grp):
                cp.start()

    @pl.when(s < k_steps)
    def _():
        for cp in value_copies(b, s):
            cp.start(priority=1)

    @pl.when((s >= k_steps) & (b + 1 < pl.num_programs(0)))
    def _():
        for cp in key_copies(b + 1, s - k_steps):
            cp.start()

    @pl.when(s == 0)
    def _():
        kn_scr[...] = jnp.zeros(kn_scr.shape, F32)
        vn_scr[...] = jnp.zeros(vn_scr.shape, F32)
        kn_scr[0:dec_seq, :] = kn_ref[0]
        vn_scr[0:dec_seq, :] = vn_ref[0]
        for r in range(q8_scr.shape[0] // dec_seq):
            q8_scr[r * dec_seq:(r + 1) * dec_seq, :] = q_ref[0]
        q = q8_scr[...]
        qrep = jnp.broadcast_to(q[None], (rows // q.shape[0], q.shape[0], d)).reshape(rows, d)
        ridx = lax.broadcasted_iota(jnp.int32, (rows, d), 0)
        lane = lax.broadcasted_iota(jnp.int32, (rows, d), 1)
        row_head = (ridx % half) // dec_seq
        row_map = ridx // half
        match = (lane // V_HEAD_DIM == row_head) & ((lane // QK_HEAD_DIM) % 2 == row_map)
        qbd_scr[...] = jnp.where(match, qrep, 0.0).astype(BF16)
        acc_scr[...] = jnp.zeros(acc_scr.shape, F32)

    @pl.when(s < k_steps)
    def _():
        for cp in key_copies(b, s):
            cp.wait()
        for e in range(slots):
            lg_scr[s * slots + e] = jnp.dot(qbd_scr[...], kbuf[s, e].astype(BF16),
                                            preferred_element_type=F32)

    @pl.when(s == k_steps - 1)
    def _():
        lam = _diff_lambda_value(lam_ref, layer)
        lg_scr[n_pages - 1] = lg_scr[n_pages - 1] + blast_ref[...]
        lgn = lax.dot_general(qbd_scr[...], kn_scr[...].astype(BF16), (((1,), (1,)), ((), ())),
                              preferred_element_type=F32) + bnew_ref[...]
        lg = lg_scr[...]
        m = jnp.maximum(jnp.max(jnp.max(lg, axis=0), axis=-1, keepdims=True),
                        jnp.max(lgn, axis=-1, keepdims=True))
        p = jnp.exp(lg - m[None])
        pn = jnp.exp(lgn - m)
        l = jnp.sum(jnp.sum(p, axis=0), axis=-1, keepdims=True) + jnp.sum(pn, axis=-1, keepdims=True)
        inv = 1.0 / l
        p = p * inv[None]
        pn = pn * inv
        w_scr[...] = p[:, 0:half, :] - lam * p[:, half:rows, :]
        wn_scr[...] = (pn[0:half, :] - lam * pn[half:rows, :]).astype(BF16)

    @pl.when(s >= k_steps)
    def _():
        grp = s - k_steps
        for cp in value_copies(b, grp):
            cp.wait()
        for g in range(n_heads // group_heads):
            part, first = divmod(g * group_heads, VALUE_TILE_HEADS)
            acc = acc_scr[g]
            for e in range(slots):
                vflat = vbuf.at[grp, e, part].reshape(page * VALUE_TILE_HEADS, V_HEAD_DIM)
                v = jnp.concatenate(
                    [vflat[pl.ds(first + hh, page, stride=VALUE_TILE_HEADS), :].astype(BF16)
                     for hh in range(group_heads)], axis=1)
                w = w_scr[grp * slots + e, g * group_rows:(g + 1) * group_rows, :]
                acc = acc + jnp.dot(w.astype(BF16), v, preferred_element_type=F32)
            acc_scr[g] = acc

    @pl.when(s == 2 * k_steps - 1)
    def _():
        new = jnp.dot(wn_scr[...], vn_scr[...].astype(BF16), preferred_element_type=F32)
        scale = 1.0 - _lambda_init(layer)
        for h in range(n_heads):
            g, e = divmod(h, group_heads)
            cols = slice(h * V_HEAD_DIM, (h + 1) * V_HEAD_DIM)
            piece = (acc_scr[g, e * dec_seq:(e + 1) * dec_seq, e * V_HEAD_DIM:(e + 1) * V_HEAD_DIM]
                     + new[h * dec_seq:(h + 1) * dec_seq, cols])
            o_ref[0, :, cols] = _rms_rows(piece, sg_ref[...], SUBLN_EPS) * scale


def _attn_sample(q, k_new, v_new, cache_k, cache_v, page_table, blast, bnew, lam, subln, layer):
    batch, dec_seq, d = q.shape
    n_pages = page_table.shape[1]
    page = cache_k.shape[2]
    n_heads = d // V_HEAD_DIM
    rows = 2 * n_heads * dec_seq
    sub_f32 = 8
    assert sub_f32 % dec_seq == 0
    tok = pl.BlockSpec((1, dec_seq, d), lambda b, s, pt: (b, 0, 0))
    full = lambda a: pl.BlockSpec(a.shape, lambda b, s, pt: (0,) * a.ndim)
    slots = SAMPLE_PAGE_SLOTS
    assert n_pages % slots == 0
    k_steps = n_pages // slots
    v_parts = n_heads // VALUE_TILE_HEADS
    grid_spec = pltpu.PrefetchScalarGridSpec(
        num_scalar_prefetch=1,
        grid=(batch, 2 * k_steps),
        in_specs=[tok, tok, tok,
                  pl.BlockSpec(memory_space=pl.ANY), pl.BlockSpec(memory_space=pl.ANY),
                  full(blast), full(bnew), full(lam),
                  pl.BlockSpec((1, V_HEAD_DIM), lambda b, s, pt: (0, 0))],
        out_specs=pl.BlockSpec((1, dec_seq, d), lambda b, s, pt: (b, 0, 0)),
        scratch_shapes=[pltpu.VMEM((rows, d), BF16),
                        pltpu.VMEM((n_pages, rows, page), F32),
                        pltpu.VMEM((n_pages, rows // 2, page), F32),
                        pltpu.VMEM((rows // 2, page), BF16),
                        pltpu.VMEM((n_heads * dec_seq // sub_f32, sub_f32,
                                    sub_f32 // dec_seq * V_HEAD_DIM), F32),
                        pltpu.VMEM((sub_f32, d), F32),
                        pltpu.VMEM((page, d), F32),
                        pltpu.VMEM((page, d), F32),
                        pltpu.VMEM((k_steps, slots, d, page), F32),
                        pltpu.VMEM((k_steps, slots, v_parts, page, VALUE_TILE_HEADS, V_HEAD_DIM), F32),
                        pltpu.SemaphoreType.DMA((k_steps,)),
                        pltpu.SemaphoreType.DMA((k_steps,))],
    )
    vmem = 2 * k_steps * slots * page * d * 4 + 6 * n_pages * rows * page * 4 + 4 * page * d * 2
    return pl.pallas_call(
        functools.partial(_attn_sample_kernel, layer=layer, n_pages=n_pages, dec_seq=dec_seq, slots=slots),
        grid_spec=grid_spec,
        out_shape=jax.ShapeDtypeStruct((batch, dec_seq, d), F32),
        compiler_params=_params(("arbitrary", "arbitrary"), vmem),
        name="attn_sample",
    )(page_table, q, k_new, v_new, cache_k, cache_v, blast, bnew, lam, subln[None, :])


PROMPT_TM = 512
ATTN_TILE = 512
PROJ_TN = 512
RES_TN = 2048
FFN_TF = 1024
FFN_CAST_TF = 512
SAMPLE_PAGE_SLOTS = 8
ATTN_HEADS_PER_STEP = 4
ATTN_SCORES_AHEAD = 4


def kernel(x_prompt, x_sample, cache_k, cache_v, page_table, rel_bias_table, mix_norm, ffn_norm,
           attn_w_qkv, attn_w_o, attn_q_norm, attn_k_norm, attn_lambda, attn_subln,
           cm_w_in, cm_v_norm, cm_w_s, cm_b_s, cm_w_o, ffn_w1, ffn_w2):
    batch, seq, d = x_prompt.shape
    dec_batch, dec_seq, _ = x_sample.shape
    depth = mix_norm.shape[0]
    n_heads = d // V_HEAD_DIM
    n_groups = cm_w_s.shape[1]
    page = cache_k.shape[2]
    mp, ms = batch * seq, dec_batch * dec_seq
    assert seq % ATTN_TILE == 0 and mp % PROMPT_TM == 0 and ms % CHUNK == 0 and CHUNK % dec_seq == 0

    xp = x_prompt.reshape(mp, d)
    xs = x_sample.reshape(ms, d)
    ck = jnp.transpose(cache_k, (0, 1, 3, 4, 5, 2)).reshape(cache_k.shape[0], cache_k.shape[1], d, page)
    cv = cache_v

    bias_p = _prompt_bias(rel_bias_table, n_heads)
    blast, bnew = _sample_bias(rel_bias_table, n_heads, dec_seq, page)

    k_prompt, v_prompt, k_sample, v_sample, cv_sample = [], [], [], [], []
    for i in range(depth):
        j = i // N_MIXERS
        if i % N_MIXERS == 0:
            w_qkv = attn_w_qkv[j].astype(BF16)
            w_o = attn_w_o[j].astype(BF16)
            qtp, kbp, ktp, vp, vtp = _qkv(xp, mix_norm[i], w_qkv, attn_q_norm[j], attn_k_norm[j],
                                          ATTN_TILE, PROJ_TN, seq=seq)
            qs, ks, vs = _qkv(xs, mix_norm[i], w_qkv, attn_q_norm[j], attn_k_norm[j], ms, PROJ_TN)
            op = _attn_prompt(qtp, kbp, vtp, bias_p, attn_lambda[j], attn_subln[j], batch, i)
            os_ = _attn_sample(qs.reshape(dec_batch, dec_seq, d),
                               ks.reshape(dec_batch, dec_seq, d), vs.reshape(dec_batch, dec_seq, d),
                               ck[j], cv[j], page_table, blast, bnew, attn_lambda[j], attn_subln[j], i)
            xp = _mm_res(xp, op, w_o, PROMPT_TM, RES_TN)
            xs = _mm_res(xs, os_.reshape(ms, d), w_o, ms, RES_TN)
            k_prompt.append(jnp.transpose(ktp.reshape(batch, n_heads, 2, QK_HEAD_DIM, seq), (0, 4, 1, 2, 3)))
            v_prompt.append(vp.reshape(batch, seq, n_heads, V_HEAD_DIM))
            k_sample.append(ks.reshape(dec_batch, dec_seq, n_heads, 2, QK_HEAD_DIM))
            v_sample.append(vs.reshape(dec_batch, dec_seq, n_heads, V_HEAD_DIM))
        else:
            w_in = cm_w_in[j].astype(BF16)
            w_o = cm_w_o[j].astype(BF16)
            bs_p = jnp.broadcast_to(cm_b_s[j][:, :, None], (n_groups, CHUNK, GROUP_DIM))
            gp, _ = _gmlp_front(xp, mix_norm[i], w_in, cm_v_norm[j], cm_w_s[j], bs_p,
                                PROMPT_TM, PROJ_TN, False)
            reps = CHUNK // dec_seq
            eye = jnp.eye(reps, dtype=F32)
            ws_s = (eye[None, :, None, :, None] * cm_w_s[j][:, None, :dec_seq, None, :dec_seq]
                    ).reshape(n_groups, CHUNK, CHUNK)
            bs_s = jnp.broadcast_to(jnp.tile(cm_b_s[j][:, :dec_seq], (1, reps))[:, :, None],
                                    (n_groups, CHUNK, GROUP_DIM))
            gs, cvs = _gmlp_front(xs, mix_norm[i], w_in, cm_v_norm[j], ws_s, bs_s, ms, PROJ_TN, True)
            xp = _mm_res(xp, gp, w_o, PROMPT_TM, RES_TN)
            xs = _mm_res(xs, gs, w_o, ms, RES_TN)
            cv_sample.append(cvs.reshape(dec_batch, dec_seq, n_groups, GROUP_DIM))
        xs, w1, w2 = _ffn(xs, ffn_norm[i], ffn_w1, ffn_w2, ms, FFN_CAST_TF, layer=i)
        xp = _ffn(xp, ffn_norm[i], w1, w2, PROMPT_TM, FFN_TF)

    return (xp.reshape(batch, seq, d), xs.reshape(dec_batch, dec_seq, d),
            jnp.stack(k_prompt), jnp.stack(v_prompt), jnp.stack(k_sample), jnp.stack(v_sample),
            jnp.stack(cv_sample))
```
